```python
import math
import jax
import jax.numpy as jnp
from jax import lax
import numpy as np

D_MODEL = 1024
BATCH = 2
SEQ = 16384
DEPTH = 4
DEC_BATCH = 16
DEC_SEQ = 16
PAST_LEN = 2048

CHUNK = 64
SSM_WIDTH = 512
SSM_GROUP = 16
SSM_GROUPS = SSM_WIDTH // SSM_GROUP
SSM_STATE = 64
N_HEADS = 8
N_KV_HEADS = 2
HEAD_DIM = 64
Q_PER_KV = N_HEADS // N_KV_HEADS
ATTN_WIDTH = N_HEADS * HEAD_DIM
KV_WIDTH = N_KV_HEADS * HEAD_DIM
WINDOW = 128
WINDOW_CHUNKS = WINDOW // CHUNK
BAND = (WINDOW_CHUNKS + 1) * CHUNK
ROPE_DIM = HEAD_DIM // 4
ROPE_THETA = 500000.0
N_BRANCHES = 2
IN_COLS = SSM_WIDTH + ATTN_WIDTH + 2 * KV_WIDTH + N_BRANCHES * D_MODEL
D_FF = 2816
CONV_WIDTH = 3
N_ADA = 6
RMS_EPS = 1e-6
NEG_INF = -1e30

kernel_name = 'hybrid_s5_swa_sink_convffn_stream_step'


def rms_norm(x, g):
    xf = x.astype(jnp.float32)
    y = xf * lax.rsqrt(jnp.mean(xf * xf, axis=-1, keepdims=True) + RMS_EPS)
    return (y * g.astype(jnp.float32)).astype(x.dtype)


def modulate(x, shift, scale):
    return x * (1 + scale[:, None, :]) + shift[:, None, :]


def rotary(x, pos):
    half = ROPE_DIM // 2
    inv_freq = ROPE_THETA ** (-(jnp.arange(half, dtype=jnp.float32) * 2.0) / ROPE_DIM)
    ang = pos.astype(jnp.float32)[:, None] * inv_freq[None, :]
    cos = jnp.cos(ang)[:, None, :]
    sin = jnp.sin(ang)[:, None, :]
    xf = x.astype(jnp.float32)
    x1 = xf[..., :half]
    x2 = xf[..., half:ROPE_DIM]
    out = jnp.concatenate([x1 * cos - x2 * sin, x2 * cos + x1 * sin, xf[..., ROPE_DIM:]], axis=-1)
    return out.astype(x.dtype)


def sink_softmax(s, sink):
    m = jnp.maximum(jnp.max(s, axis=-1, keepdims=True), sink)
    p = jnp.exp(s - m)
    return p / (jnp.sum(p, axis=-1, keepdims=True) + jnp.exp(sink - m))


def banded_attention(q, k, v, sink):
    b, l = q.shape[0], q.shape[1]
    nc = l // CHUNK
    qb = q.reshape(b, nc, CHUNK, N_KV_HEADS, Q_PER_KV, HEAD_DIM).astype(jnp.float32)

    def band(t):
        tp = jnp.pad(t, ((0, 0), (WINDOW, 0), (0, 0), (0, 0)))
        tp = tp.reshape(b, nc + WINDOW_CHUNKS, CHUNK, N_KV_HEADS, HEAD_DIM)
        return jnp.concatenate([tp[:, j:j + nc] for j in range(WINDOW_CHUNKS + 1)], axis=2)

    kb = band(k).astype(jnp.float32)
    vb = band(v).astype(jnp.float32)
    s = jnp.einsum('bnqhgd,bnkhd->bnhgqk', qb, kb) * (HEAD_DIM ** -0.5)
    key_chunk = jnp.arange(nc)[:, None] - WINDOW_CHUNKS + jnp.arange(BAND)[None, :] // CHUNK
    valid = key_chunk >= 0
    s = jnp.where(valid[None, :, None, None, None, :], s, NEG_INF)
    sk = sink.astype(jnp.float32).reshape(N_KV_HEADS, Q_PER_KV)[None, None, :, :, None, None]
    p = sink_softmax(s, sk)
    o = jnp.einsum('bnhgqk,bnkhd->bnqhgd', p, vb)
    return o.reshape(b, l, ATTN_WIDTH).astype(q.dtype)


def cached_attention(q, k, v, past_k, past_v, sink):
    b, t = q.shape[0], q.shape[1]
    qs = q.reshape(b, t, N_KV_HEADS, Q_PER_KV, HEAD_DIM).astype(jnp.float32)
    k_all = jnp.concatenate([past_k, k], axis=1).astype(jnp.float32)
    v_all = jnp.concatenate([past_v, v], axis=1).astype(jnp.float32)
    s = jnp.einsum('bqhgd,bkhd->bhgqk', qs, k_all) * (HEAD_DIM ** -0.5)
    sk = sink.astype(jnp.float32).reshape(N_KV_HEADS, Q_PER_KV)[None, :, :, None, None]
    p = sink_softmax(s, sk)
    o = jnp.einsum('bhgqk,bkhd->bqhgd', p, v_all)
    return o.reshape(b, t, ATTN_WIDTH).astype(q.dtype)


def ssm_discretise(a_re, a_im, log_dt, b_re, b_im):
    a_re = a_re.astype(jnp.float32)
    a_im = a_im.astype(jnp.float32)
    dt = jnp.exp(log_dt.astype(jnp.float32))[:, None]
    mag = jnp.exp(a_re * dt)
    abar_re = mag * jnp.cos(a_im * dt)
    abar_im = mag * jnp.sin(a_im * dt)
    nr = abar_re - 1.0
    ni = abar_im
    den = a_re * a_re + a_im * a_im
    fr = (nr * a_re + ni * a_im) / den
    fi = (ni * a_re - nr * a_im) / den
    b_re = b_re.astype(jnp.float32)
    b_im = b_im.astype(jnp.float32)
    bbar_re = fr[..., None] * b_re - fi[..., None] * b_im
    bbar_im = fr[..., None] * b_im + fi[..., None] * b_re
    return abar_re, abar_im, bbar_re, bbar_im


def ssm_combine(e1, e2):
    a1r, a1i, b1r, b1i = e1
    a2r, a2i, b2r, b2i = e2
    return (a2r * a1r - a2i * a1i,
            a2r * a1i + a2i * a1r,
            a2r * b1r - a2i * b1i + b2r,
            a2r * b1i + a2i * b1r + b2i)


def ssm_mixer(u, s0_re, s0_im, a_re, a_im, log_dt, b_re, b_im, c_re, c_im, d_skip):
    b, l = u.shape[0], u.shape[1]
    uf = u.astype(jnp.float32)
    ug = uf.reshape(b, l, SSM_GROUPS, SSM_GROUP)
    abr, abi, bbr, bbi = ssm_discretise(a_re, a_im, log_dt, b_re, b_im)
    xr = jnp.einsum('gpc,blgc->blgp', bbr, ug)
    xi = jnp.einsum('gpc,blgc->blgp', bbi, ug)
    ar = jnp.broadcast_to(abr, (1, l, SSM_GROUPS, SSM_STATE))
    ai = jnp.broadcast_to(abi, (1, l, SSM_GROUPS, SSM_STATE))
    pr, pi, hr, hi = lax.associative_scan(ssm_combine, (ar, ai, xr, xi), axis=1)
    s0r = s0_re.astype(jnp.float32)[:, None]
    s0i = s0_im.astype(jnp.float32)[:, None]
    sr = hr + pr * s0r - pi * s0i
    si = hi + pr * s0i + pi * s0r
    y = (jnp.einsum('gcp,blgp->blgc', c_re.astype(jnp.float32), sr)
         - jnp.einsum('gcp,blgp->blgc', c_im.astype(jnp.float32), si))
    y = y.reshape(b, l, SSM_WIDTH) + d_skip.astype(jnp.float32) * uf
    return y.astype(u.dtype), sr[:, -1], si[:, -1]


def causal_dwconv(h, prev, w, bias):
    l = h.shape[1]
    hp = jnp.concatenate([prev.astype(h.dtype), h], axis=1)
    out = bias + hp[:, 0:l] * w[0]
    for j in range(1, CONV_WIDTH):
        out = out + hp[:, j:j + l] * w[j]
    return out, hp[:, -(CONV_WIDTH - 1):]


def block(x, c, pos, past_k, past_v, s0_re, s0_im, conv_prev,
          ada_w, ada_b, norm1_g, norm2_g, w_in, a_re, a_im, log_dt, b_re, b_im,
          c_re, c_im, d_skip, w_glu, sink, w_attn_o, w_out, w_up, conv_w, conv_b, w_down):
    b, l = x.shape[0], x.shape[1]
    mod = jnp.einsum('bd,de->be', jax.nn.silu(c), ada_w) + ada_b
    sh1, sc1, g1, sh2, sc2, g2 = jnp.split(mod, N_ADA, axis=-1)
    h = modulate(rms_norm(x, norm1_g), sh1, sc1)
    proj = h @ w_in
    o1 = SSM_WIDTH
    o2 = o1 + ATTN_WIDTH
    o3 = o2 + KV_WIDTH
    o4 = o3 + KV_WIDTH
    u, q, k, v, gates = jnp.split(proj, [o1, o2, o3, o4], axis=-1)
    ga, gb = jnp.split(gates, N_BRANCHES, axis=-1)
    y_ssm, s_re, s_im = ssm_mixer(u, s0_re, s0_im, a_re, a_im, log_dt, b_re, b_im, c_re, c_im, d_skip)
    glu_a, glu_b = jnp.split(jax.nn.gelu(y_ssm) @ w_glu, 2, axis=-1)
    branch_a = glu_a * jax.nn.sigmoid(glu_b)
    q = rotary(q.reshape(b, l, N_HEADS, HEAD_DIM), pos)
    k = rotary(k.reshape(b, l, N_KV_HEADS, HEAD_DIM), pos)
    v = v.reshape(b, l, N_KV_HEADS, HEAD_DIM)
    if past_k is None:
        o = banded_attention(q, k, v, sink)
        new_k = k[:, -WINDOW:]
        new_v = v[:, -WINDOW:]
    else:
        o = cached_attention(q, k, v, past_k, past_v, sink)
        new_k = k
        new_v = v
    branch_b = o @ w_attn_o
    merged = jax.nn.sigmoid(ga) * branch_a + jax.nn.sigmoid(gb) * branch_b
    x = x + g1[:, None, :] * (merged @ w_out)
    h = modulate(rms_norm(x, norm2_g), sh2, sc2)
    up, conv_state = causal_dwconv(h @ w_up, conv_prev, conv_w, conv_b)
    val, gate = jnp.split(up, 2, axis=-1)
    x = x + g2[:, None, :] * ((jax.nn.silu(gate) * val) @ w_down)
    dt = x.dtype
    return x, (new_k.astype(dt), new_v.astype(dt), s_re.astype(dt), s_im.astype(dt), conv_state.astype(dt))


def stack_layers(states, i):
    return jnp.stack([s[i] for s in states], axis=0)


def setup_inputs(seed: int = 0) -> dict:
    key = jax.random.key(seed)
    ks = iter(jax.random.split(key, 40))
    f32 = jnp.float32

    def nrm(shape, scale):
        return jax.random.normal(next(ks), shape, f32) * scale

    n_idx = jnp.arange(SSM_STATE, dtype=f32)
    return {
        'x_prompt': nrm((BATCH, SEQ, D_MODEL), 1.0),
        'x_sample': nrm((DEC_BATCH, DEC_SEQ, D_MODEL), 1.0),
        'cache_k': nrm((DEPTH, DEC_BATCH, WINDOW, N_KV_HEADS, HEAD_DIM), 1.0),
        'cache_v': nrm((DEPTH, DEC_BATCH, WINDOW, N_KV_HEADS, HEAD_DIM), 1.0),
        'state_ssm_re': nrm((DEPTH, DEC_BATCH, SSM_GROUPS, SSM_STATE), 0.5),
        'state_ssm_im': nrm((DEPTH, DEC_BATCH, SSM_GROUPS, SSM_STATE), 0.5),
        'state_conv': nrm((DEPTH, DEC_BATCH, CONV_WIDTH - 1, 2 * D_FF), 1.0),
        'c_prompt': nrm((BATCH, D_MODEL), 1.0),
        'c_sample': nrm((DEC_BATCH, D_MODEL), 1.0),
        'ada_w': nrm((DEPTH, D_MODEL, N_ADA * D_MODEL), 0.5 * D_MODEL ** -0.5),
        'ada_b': nrm((DEPTH, N_ADA * D_MODEL), 0.01),
        'norm1_g': 1.0 + nrm((DEPTH, D_MODEL), 0.01),
        'norm2_g': 1.0 + nrm((DEPTH, D_MODEL), 0.01),
        'w_in': nrm((DEPTH, D_MODEL, IN_COLS), D_MODEL ** -0.5),
        'ssm_a_re': -0.5 + nrm((DEPTH, SSM_GROUPS, SSM_STATE), 0.01),
        'ssm_a_im': math.pi * n_idx + nrm((DEPTH, SSM_GROUPS, SSM_STATE), 0.01),
        'ssm_log_dt': jax.random.uniform(next(ks), (DEPTH, SSM_GROUPS), f32, math.log(1e-3), math.log(1e-1)),
        'ssm_b_re': nrm((DEPTH, SSM_GROUPS, SSM_STATE, SSM_GROUP), (2 * SSM_GROUP) ** -0.5),
        'ssm_b_im': nrm((DEPTH, SSM_GROUPS, SSM_STATE, SSM_GROUP), (2 * SSM_GROUP) ** -0.5),
        'ssm_c_re': nrm((DEPTH, SSM_GROUPS, SSM_GROUP, SSM_STATE), SSM_STATE ** -0.5),
        'ssm_c_im': nrm((DEPTH, SSM_GROUPS, SSM_GROUP, SSM_STATE), SSM_STATE ** -0.5),
        'ssm_d': nrm((DEPTH, SSM_WIDTH), 1.0),
        'w_glu': nrm((DEPTH, SSM_WIDTH, 2 * D_MODEL), SSM_WIDTH ** -0.5),
        'attn_sink': nrm((DEPTH, N_HEADS), 0.5),
        'w_attn_o': nrm((DEPTH, ATTN_WIDTH, D_MODEL), ATTN_WIDTH ** -0.5),
        'w_out': nrm((DEPTH, D_MODEL, D_MODEL), D_MODEL ** -0.5),
        'ffn_w_up': nrm((DEPTH, D_MODEL, 2 * D_FF), D_MODEL ** -0.5),
        'ffn_conv_w': nrm((DEPTH, CONV_WIDTH, 2 * D_FF), CONV_WIDTH ** -0.5),
        'ffn_conv_b': nrm((DEPTH, 2 * D_FF), 0.01),
        'ffn_w_down': nrm((DEPTH, D_FF, D_MODEL), D_FF ** -0.5),
        'final_g': 1.0 + nrm((D_MODEL,), 0.01),
    }


def reference(x_prompt, x_sample, cache_k, cache_v, state_ssm_re, state_ssm_im, state_conv,
              c_prompt, c_sample, ada_w, ada_b, norm1_g, norm2_g, w_in, ssm_a_re, ssm_a_im,
              ssm_log_dt, ssm_b_re, ssm_b_im, ssm_c_re, ssm_c_im, ssm_d, w_glu, attn_sink,
              w_attn_o, w_out, ffn_w_up, ffn_conv_w, ffn_conv_b, ffn_w_down, final_g):
    bp, lp = x_prompt.shape[0], x_prompt.shape[1]
    ls = x_sample.shape[1]
    pos_p = jnp.arange(lp)
    pos_s = PAST_LEN + jnp.arange(ls)
    zero_ssm = jnp.zeros((bp, SSM_GROUPS, SSM_STATE), x_prompt.dtype)
    zero_conv = jnp.zeros((bp, CONV_WIDTH - 1, 2 * D_FF), x_prompt.dtype)
    xp = x_prompt
    xs = x_sample
    st_p = []
    st_s = []
    for i in range(DEPTH):
        w = (ada_w[i], ada_b[i], norm1_g[i], norm2_g[i], w_in[i], ssm_a_re[i], ssm_a_im[i],
             ssm_log_dt[i], ssm_b_re[i], ssm_b_im[i], ssm_c_re[i], ssm_c_im[i], ssm_d[i],
             w_glu[i], attn_sink[i], w_attn_o[i], w_out[i], ffn_w_up[i], ffn_conv_w[i],
             ffn_conv_b[i], ffn_w_down[i])
        xp, sp = block(xp, c_prompt, pos_p, None, None, zero_ssm, zero_ssm, zero_conv, *w)
        xs, ss = block(xs, c_sample, pos_s, cache_k[i], cache_v[i], state_ssm_re[i],
                       state_ssm_im[i], state_conv[i], *w)
        st_p.append(sp)
        st_s.append(ss)
    y_prompt = rms_norm(xp, final_g)
    y_sample = rms_norm(xs, final_g)
    return (y_prompt, y_sample,
            stack_layers(st_p, 0), stack_layers(st_p, 1), stack_layers(st_p, 2),
            stack_layers(st_p, 3), stack_layers(st_p, 4),
            stack_layers(st_s, 0), stack_layers(st_s, 1), stack_layers(st_s, 2),
            stack_layers(st_s, 3), stack_layers(st_s, 4))
```

```python
import functools
import math

import jax
import jax.numpy as jnp
from jax import lax
from jax.experimental import pallas as pl
from jax.experimental.pallas import tpu as pltpu

CHUNK = 64
SSM_GROUP = 16
SSM_STATE = 64
N_HEADS = 8
N_KV_HEADS = 2
HEAD_DIM = 64
WINDOW = 128
ROPE_DIM = HEAD_DIM // 4
ROPE_THETA = 500000.0
CONV_WIDTH = 3
N_ADA = 6
RMS_EPS = 1e-6
NEG_INF = -1e30
PAST_LEN = 2048

LANES = 128
SUBLANES = 8
MXU_DIM = 256
VMEM_LIMIT_BYTES = 58 * 1024 * 1024

PROMPT_TILE = 512
SCAN_LANES = 512
BF16 = jnp.bfloat16
F32 = jnp.float32


def _dot(a, b):
    return jnp.dot(a, b, preferred_element_type=F32)


def _dot_nt(a, b):
    return lax.dot_general(a, b, (((1,), (1,)), ((), ())), preferred_element_type=F32)


def _sigmoid(x):
    return 1.0 / (1.0 + jnp.exp(-x))


def _gelu_tanh(x):
    c = math.sqrt(2.0 / math.pi)
    return 0.5 * x * (1.0 + jnp.tanh(c * (x + 0.044715 * (x * x * x))))


def _rms_modulate(x3, g_row, shift3, scale3):
    y = x3 * lax.rsqrt(jnp.mean(x3 * x3, axis=-1, keepdims=True) + RMS_EPS)
    y = y * g_row[None]
    return y * (1.0 + scale3) + shift3


def _single(shape, index_map):
    return pl.BlockSpec(shape, index_map, pipeline_mode=pl.Buffered(1))


def _ada_kernel(c_ref, w_ref, b_ref, o_ref):
    c = c_ref[...]
    a = (c * _sigmoid(c)).astype(BF16)
    o_ref[0] = _dot(a, w_ref[0].astype(BF16)) + b_ref[0]


def _ada_modulation(c_all, ada_w, ada_b):
    depth, d, e = ada_w.shape
    nb = c_all.shape[0]
    col = d
    return pl.pallas_call(
        _ada_kernel,
        grid=(depth, e // col),
        in_specs=[
            pl.BlockSpec((nb, d), lambda l, j: (0, 0)),
            pl.BlockSpec((1, d, col), lambda l, j: (l, 0, j)),
            pl.BlockSpec((1, 1, col), lambda l, j: (l, 0, j)),
        ],
        out_specs=pl.BlockSpec((1, nb, col), lambda l, j: (l, 0, j)),
        out_shape=jax.ShapeDtypeStruct((depth, nb, e), F32),
        name="ada_modulation",
    )(c_all, ada_w, ada_b.reshape(depth, 1, e))


def _discretise_kernel(are_ref, aim_ref, ldt_ref, bre_ref, bim_ref,
                       bbr_ref, bbi_ref, pwr_ref, pwi_ref, *, n_pow):
    a_re = are_ref[0]
    a_im = aim_ref[0]
    dt = jnp.exp(ldt_ref[0])
    mag = jnp.exp(a_re * dt)
    abar_re = mag * jnp.cos(a_im * dt)
    abar_im = mag * jnp.sin(a_im * dt)
    nr = abar_re - 1.0
    ni = abar_im
    den = a_re * a_re + a_im * a_im
    fr = (nr * a_re + ni * a_im) / den
    fi = (ni * a_re - nr * a_im) / den
    b_re = bre_ref[0]
    b_im = bim_ref[0]
    bbr_ref[0] = fr[:, None, :] * b_re - fi[:, None, :] * b_im
    bbi_ref[0] = fr[:, None, :] * b_im + fi[:, None, :] * b_re
    pr, pi = abar_re, abar_im
    pwr_ref[0, 0] = pr
    pwi_ref[0, 0] = pi
    for t in range(1, n_pow):
        pr, pi = abar_re * pr - abar_im * pi, abar_re * pi + abar_im * pr
        pwr_ref[0, t] = pr
        pwi_ref[0, t] = pi


def _discretise(a_re, a_im, log_dt, b_re, b_im, n_pow):
    depth, g, p = a_re.shape
    gc = b_re.shape[-1]
    b_re_t = jnp.swapaxes(b_re, 2, 3)
    b_im_t = jnp.swapaxes(b_im, 2, 3)
    spec_a = pl.BlockSpec((1, g, p), lambda l: (l, 0, 0))
    spec_b = pl.BlockSpec((1, g, gc, p), lambda l: (l, 0, 0, 0))
    spec_p = pl.BlockSpec((1, n_pow, g, p), lambda l: (l, 0, 0, 0))
    return pl.pallas_call(
        functools.partial(_discretise_kernel, n_pow=n_pow),
        grid=(depth,),
        in_specs=[spec_a, spec_a, pl.BlockSpec((1, g, 1), lambda l: (l, 0, 0)), spec_b, spec_b],
        out_specs=[spec_b, spec_b, spec_p, spec_p],
        out_shape=[jax.ShapeDtypeStruct((depth, g, gc, p), F32)] * 2
        + [jax.ShapeDtypeStruct((depth, n_pow, g, p), F32)] * 2,
        name="s5_discretise",
    )(a_re, a_im, log_dt.reshape(depth, g, 1), b_re_t, b_im_t)


def _block_diag_in(bbar, groups_per_block):
    g, gc, p = bbar.shape
    nb = g // groups_per_block
    eye = jnp.eye(groups_per_block, dtype=bbar.dtype)
    b = bbar.reshape(nb, groups_per_block, gc, p)
    out = b[:, :, :, None, :] * eye[None, :, None, :, None]
    return out.reshape(nb, groups_per_block * gc, groups_per_block * p)


def _block_diag_out(c, groups_per_block):
    g, gc, p = c.shape
    nb = g // groups_per_block
    eye = jnp.eye(groups_per_block, dtype=c.dtype)
    ct = jnp.swapaxes(c, 1, 2).reshape(nb, groups_per_block, p, gc)
    out = ct[:, :, :, None, :] * eye[None, :, None, :, None]
    return out.reshape(nb, groups_per_block * p, groups_per_block * gc)


def _mixer_kernel(x_ref, mod_ref, g1_ref, win_ref, wgate_ref, bre_ref, bim_ref, cre_ref, cim_ref,
                  pwr_ref, pwi_ref, dskip_ref, wglu_ref, sink_ref, wao_ref, wout_ref,
                  cos_ref, sina_ref, sinb_ref, s0r_ref, s0i_ref, pk_ref, pv_ref,
                  xo_ref, ko_ref, vo_ref, sro_ref, sio_ref,
                  h_ref, u_ref, up_ref, xr_ref, xi_ref, yp_ref, sinr_ref, sini_ref,
                  carr_ref, cari_ref, kb_ref, vb_ref, kd_ref, vd_ref, ao_ref,
                  *, nb, tl, nch, t_steps, n_tiles, cached):
    tm = nb * tl
    d = x_ref.shape[-1]
    ssm_w = up_ref.shape[-1]
    n_state = xr_ref.shape[-1]
    attn_w = N_HEADS * HEAD_DIM
    kv_w = N_KV_HEADS * HEAD_DIM
    tile = pl.program_id(0) % n_tiles
    first = tile == 0

    x3 = x_ref[...]
    mod = mod_ref[...]
    h3 = _rms_modulate(x3, g1_ref[...], mod[:, :, 0:d], mod[:, :, d:2 * d])
    hb = h3.reshape(tm, d).astype(BF16)
    h_ref[...] = hb
    proj = _dot(hb, win_ref[...])
    u = proj[:, 0:ssm_w]
    for c in range(ssm_w // LANES):
        u_ref[c] = u[:, c * LANES:(c + 1) * LANES]
    q = proj[:, ssm_w:ssm_w + attn_w]
    k = proj[:, ssm_w + attn_w:ssm_w + attn_w + kv_w]
    v = proj[:, ssm_w + attn_w + kv_w:ssm_w + attn_w + 2 * kv_w]

    for t in range(t_steps):
        for c in range(ssm_w // LANES):
            up_ref[t * nch:(t + 1) * nch, c * LANES:(c + 1) * LANES] = u_ref[c, pl.ds(t, nch, stride=t_steps), :]
    n_in_blocks = bre_ref.shape[0]
    kin = ssm_w // n_in_blocks
    nin = n_state // n_in_blocks
    for j in range(n_in_blocks):
        ub = up_ref[:, j * kin:(j + 1) * kin].astype(BF16)
        xr_ref[:, j * nin:(j + 1) * nin] = _dot(ub, bre_ref[j])
        xi_ref[:, j * nin:(j + 1) * nin] = _dot(ub, bim_ref[j])

    def scan_block(lo, init_r, init_i, store):
        a_r = jnp.broadcast_to(pwr_ref[0:1, lo:lo + SCAN_LANES], (nch, SCAN_LANES))
        a_i = jnp.broadcast_to(pwi_ref[0:1, lo:lo + SCAN_LANES], (nch, SCAN_LANES))

        def step(t, carry):
            s_r, s_i = carry
            row = pl.multiple_of(t * nch, nch)
            in_r = xr_ref[pl.ds(row, nch), lo:lo + SCAN_LANES]
            in_i = xi_ref[pl.ds(row, nch), lo:lo + SCAN_LANES]
            n_r = a_r * s_r - a_i * s_i + in_r
            n_i = a_r * s_i + a_i * s_r + in_i
            if store:
                xr_ref[pl.ds(row, nch), lo:lo + SCAN_LANES] = n_r
                xi_ref[pl.ds(row, nch), lo:lo + SCAN_LANES] = n_i
            return n_r, n_i

        return lax.fori_loop(0, t_steps, step, (init_r, init_i), unroll=2)

    if not cached:
        @pl.when(first)
        def _():
            carr_ref[...] = s0r_ref[0]
            cari_ref[...] = s0i_ref[0]

        zero = jnp.zeros((nch, SCAN_LANES), F32)
        for lo in range(0, n_state, SCAN_LANES):
            f_r, f_i = scan_block(lo, zero, zero, store=False)
            at_r = pwr_ref[t_steps - 1:t_steps, lo:lo + SCAN_LANES]
            at_i = pwi_ref[t_steps - 1:t_steps, lo:lo + SCAN_LANES]
            c_r = carr_ref[:, lo:lo + SCAN_LANES]
            c_i = cari_ref[:, lo:lo + SCAN_LANES]
            for ch in range(nch):
                sinr_ref[ch:ch + 1, lo:lo + SCAN_LANES] = c_r
                sini_ref[ch:ch + 1, lo:lo + SCAN_LANES] = c_i
                c_r, c_i = (at_r * c_r - at_i * c_i + f_r[ch:ch + 1],
                            at_r * c_i + at_i * c_r + f_i[ch:ch + 1])
            carr_ref[:, lo:lo + SCAN_LANES] = c_r
            cari_ref[:, lo:lo + SCAN_LANES] = c_i
        sro_ref[0] = carr_ref[...]
        sio_ref[0] = cari_ref[...]
    else:
        sinr_ref[...] = s0r_ref[0]
        sini_ref[...] = s0i_ref[0]

    for lo in range(0, n_state, SCAN_LANES):
        f_r, f_i = scan_block(lo, sinr_ref[:, lo:lo + SCAN_LANES], sini_ref[:, lo:lo + SCAN_LANES],
                              store=True)
        if cached:
            sro_ref[0, :, lo:lo + SCAN_LANES] = f_r
            sio_ref[0, :, lo:lo + SCAN_LANES] = f_i

    n_out_blocks = cre_ref.shape[0]
    kout = n_state // n_out_blocks
    nout = ssm_w // n_out_blocks
    for j in range(n_out_blocks):
        sr = xr_ref[:, j * kout:(j + 1) * kout].astype(BF16)
        si = xi_ref[:, j * kout:(j + 1) * kout].astype(BF16)
        yb = _dot(sr, cre_ref[j]) - _dot(si, cim_ref[j])
        for c in range(nout // LANES):
            yp_ref[j * (nout // LANES) + c] = yb[:, c * LANES:(c + 1) * LANES]
    y = jnp.concatenate(
        [jnp.concatenate([yp_ref[c, pl.ds(ch, t_steps, stride=nch), :] for ch in range(nch)], axis=0)
         for c in range(ssm_w // LANES)], axis=1)
    y = y + dskip_ref[...] * u
    glu = _dot(_gelu_tanh(y).astype(BF16), wglu_ref[...])
    branch_a = glu[:, 0:d] * _sigmoid(glu[:, d:2 * d])

    cos3 = cos_ref[...]
    sina3 = sina_ref[...]
    sinb3 = sinb_ref[...]

    def rope(z):
        za = pltpu.roll(z, LANES - ROPE_DIM // 2, axis=1).reshape(nb, tl, LANES)
        zb = pltpu.roll(z, ROPE_DIM // 2, axis=1).reshape(nb, tl, LANES)
        out = z.reshape(nb, tl, LANES) * cos3 + za * sina3 + zb * sinb3
        return out.reshape(tm, LANES)

    k_rot = rope(k)
    lane = lax.broadcasted_iota(jnp.int32, (1, LANES), 1)
    low_half = lane < HEAD_DIM

    def both_halves(z2):
        zs = pltpu.roll(z2, HEAD_DIM, axis=1)
        return jnp.where(low_half, z2, zs), jnp.where(low_half, zs, z2)

    kr = WINDOW if not cached else tl
    k3 = k_rot.reshape(nb, tl, LANES)
    v3 = v.reshape(nb, tl, LANES)
    ko_ref[...] = k3[:, tl - kr:, :]
    vo_ref[...] = v3[:, tl - kr:, :]
    if not cached:
        @pl.when(first)
        def _():
            kb_ref[0:WINDOW, :] = jnp.zeros((WINDOW, LANES), F32)
            vb_ref[0:WINDOW, :] = jnp.zeros((WINDOW, LANES), F32)

        kb_ref[WINDOW:WINDOW + tm, :] = k_rot
        vb_ref[WINDOW:WINDOW + tm, :] = v
        kall = kb_ref[...]
        vall = vb_ref[...]
        band = WINDOW + CHUNK
        n_rows = CHUNK
        n_blocks = tm // CHUNK
    else:
        kb_ref[:, 0:WINDOW, :] = pk_ref[...]
        kb_ref[:, WINDOW:WINDOW + tl, :] = k3
        vb_ref[:, 0:WINDOW, :] = pv_ref[...]
        vb_ref[:, WINDOW:WINDOW + tl, :] = v3
        band = WINDOW + tl
        n_rows = tl
        n_blocks = nb
        kall = kb_ref[...].reshape(nb * band, LANES)
        vall = vb_ref[...].reshape(nb * band, LANES)
    k0, k1 = both_halves(kall)
    v0, v1 = both_halves(vall)
    kd_ref[0] = k0.astype(BF16)
    kd_ref[1] = k1.astype(BF16)
    vd_ref[0] = v0.astype(BF16)
    vd_ref[1] = v1.astype(BF16)
    if not cached:
        kb_ref[0:WINDOW, :] = kb_ref[tm:tm + WINDOW, :]
        vb_ref[0:WINDOW, :] = vb_ref[tm:tm + WINDOW, :]

    scale = HEAD_DIM ** -0.5
    q_pairs = [rope(q[:, p * LANES:(p + 1) * LANES]) * scale for p in range(attn_w // LANES)]
    q_per_kv = N_HEADS // N_KV_HEADS
    key_idx = lax.broadcasted_iota(jnp.int32, (1, band), 1)
    for blk in range(n_blocks):
        r0 = blk * n_rows
        b0 = blk * (CHUNK if not cached else band)
        for j in range(N_KV_HEADS):
            pieces = []
            for p in range(q_per_kv // 2):
                qp = q_pairs[j * (q_per_kv // 2) + p][r0:r0 + n_rows]
                pieces.append(jnp.where(low_half, qp, 0.0))
                pieces.append(jnp.where(low_half, 0.0, qp))
            lhs = jnp.concatenate(pieces, axis=0).astype(BF16)
            s = _dot_nt(lhs, kd_ref[j, b0:b0 + band, :])
            if not cached and blk < WINDOW // CHUNK:
                first_valid = jnp.where(first, (WINDOW // CHUNK - blk) * CHUNK, 0)
                s = jnp.where(key_idx >= first_valid, s, NEG_INF)
            sink = jnp.concatenate(
                [jnp.full((n_rows, 1), sink_ref[j * q_per_kv + g], F32) for g in range(q_per_kv)], axis=0)
            m = jnp.maximum(jnp.max(s, axis=-1, keepdims=True), sink)
            pexp = jnp.exp(s - m)
            den = jnp.sum(pexp, axis=-1, keepdims=True) + jnp.exp(sink - m)
            o2 = _dot(pexp.astype(BF16), vd_ref[j, b0:b0 + band, :]) / den
            for p in range(q_per_kv // 2):
                lo_rows = o2[(2 * p) * n_rows:(2 * p + 1) * n_rows]
                hi_rows = o2[(2 * p + 1) * n_rows:(2 * p + 2) * n_rows]
                col = (j * (q_per_kv // 2) + p) * LANES
                ao_ref[r0:r0 + n_rows, col:col + LANES] = jnp.where(low_half, lo_rows, hi_rows).astype(BF16)
    branch_b = _dot(ao_ref[...], wao_ref[...])

    gates = _dot(h_ref[...], wgate_ref[...])
    merged = _sigmoid(gates[:, 0:d]) * branch_a + _sigmoid(gates[:, d:2 * d]) * branch_b
    out = _dot(merged.astype(BF16), wout_ref[...])
    xo_ref[...] = x3 + mod[:, :, 2 * d:3 * d] * out.reshape(nb, tl, d)


def _mixer(x, mod, g1, w_in, w_gate, b_re, b_im, c_re, c_im, pw_re, pw_im, d_skip, w_glu, sink,
           w_ao, w_out, cos_t, sina_t, sinb_t, s0_re, s0_im, past_k, past_v, *, nb, tl, cached):
    b, l, d = x.shape
    n_tiles = l // tl
    n_steps = (b // nb) * n_tiles
    tm = nb * tl
    ssm_w = d_skip.shape[-1]
    n_state = pw_re.shape[-1]
    attn_w = N_HEADS * HEAD_DIM
    if cached:
        nch, t_steps = nb, tl
    else:
        nch, t_steps = SUBLANES, tl // SUBLANES
    kr = tl if cached else WINDOW
    band_rows = (nb, WINDOW + tl, LANES) if cached else (WINDOW + tm, LANES)
    dup_rows = nb * (WINDOW + tl) if cached else WINDOW + tm

    def tile_map(i):
        return (i // n_tiles, i % n_tiles, 0)

    def batch_map(i):
        return (i // n_tiles, 0, 0)

    const2 = lambda i: (0, 0)
    const3 = lambda i: (0, 0, 0)
    in_specs = [
        pl.BlockSpec((nb, tl, d), tile_map),
        pl.BlockSpec((nb, 1, N_ADA * d), batch_map),
        _single((1, d), const2),
        _single(w_in.shape, const2),
        _single(w_gate.shape, const2),
        _single(b_re.shape, const3),
        _single(b_im.shape, const3),
        _single(c_re.shape, const3),
        _single(c_im.shape, const3),
        _single(pw_re.shape, const2),
        _single(pw_im.shape, const2),
        _single((1, ssm_w), const2),
        _single(w_glu.shape, const2),
        pl.BlockSpec(memory_space=pltpu.SMEM),
        _single(w_ao.shape, const2),
        _single(w_out.shape, const2),
        pl.BlockSpec((1, tl, LANES), lambda i: (0, i % n_tiles, 0)),
        pl.BlockSpec((1, tl, LANES), lambda i: (0, i % n_tiles, 0)),
        pl.BlockSpec((1, tl, LANES), lambda i: (0, i % n_tiles, 0)),
        pl.BlockSpec((1, s0_re.shape[1], n_state), batch_map),
        pl.BlockSpec((1, s0_im.shape[1], n_state), batch_map),
        pl.BlockSpec((nb,) + past_k.shape[1:], batch_map),
        pl.BlockSpec((nb,) + past_v.shape[1:], batch_map),
    ]
    out_specs = [
        pl.BlockSpec((nb, tl, d), tile_map),
        pl.BlockSpec((nb, kr, LANES), batch_map),
        pl.BlockSpec((nb, kr, LANES), batch_map),
        pl.BlockSpec((1, s0_re.shape[1], n_state), batch_map),
        pl.BlockSpec((1, s0_re.shape[1], n_state), batch_map),
    ]
    out_shape = [
        jax.ShapeDtypeStruct((b, l, d), F32),
        jax.ShapeDtypeStruct((b, kr, LANES), F32),
        jax.ShapeDtypeStruct((b, kr, LANES), F32),
        jax.ShapeDtypeStruct(s0_re.shape, F32),
        jax.ShapeDtypeStruct(s0_im.shape, F32),
    ]
    scratch = [
        pltpu.VMEM((tm, d), BF16),
        pltpu.VMEM((ssm_w // LANES, tm, LANES), F32),
        pltpu.VMEM((tm, ssm_w), F32),
        pltpu.VMEM((tm, n_state), F32),
        pltpu.VMEM((tm, n_state), F32),
        pltpu.VMEM((ssm_w // LANES, tm, LANES), F32),
        pltpu.VMEM((nch, n_state), F32),
        pltpu.VMEM((nch, n_state), F32),
        pltpu.VMEM((1, n_state), F32),
        pltpu.VMEM((1, n_state), F32),
        pltpu.VMEM(band_rows, F32),
        pltpu.VMEM(band_rows, F32),
        pltpu.VMEM((N_KV_HEADS, dup_rows, LANES), BF16),
        pltpu.VMEM((N_KV_HEADS, dup_rows, LANES), BF16),
        pltpu.VMEM((tm, attn_w), BF16),
    ]
    kern = functools.partial(_mixer_kernel, nb=nb, tl=tl, nch=nch, t_steps=t_steps,
                             n_tiles=n_tiles, cached=cached)
    return pl.pallas_call(
        kern,
        grid=(n_steps,),
        in_specs=in_specs,
        out_specs=out_specs,
        out_shape=out_shape,
        scratch_shapes=scratch,
        compiler_params=pltpu.CompilerParams(dimension_semantics=("arbitrary",),
                                             vmem_limit_bytes=VMEM_LIMIT_BYTES),
        name="mixer_cached" if cached else "mixer_banded",
    )(x, mod, g1, w_in, w_gate, b_re, b_im, c_re, c_im, pw_re, pw_im, d_skip, w_glu, sink,
      w_ao, w_out, cos_t, sina_t, sinb_t, s0_re, s0_im, past_k, past_v)


def _ffn_kernel(x_ref, mod_ref, g2_ref, wup_ref, cw_ref, cb_ref, wdn_ref, prev_ref, fg_ref,
                xo_ref, cs_ref, ub_ref, carry_ref, *, nb, tl, n_tiles, col_w, final):
    tm = nb * tl
    d = x_ref.shape[-1]
    d_ff = wdn_ref.shape[0]
    pad = SUBLANES
    first = pl.program_id(0) % n_tiles == 0

    @pl.when(first)
    def _():
        carry_ref[...] = prev_ref[...]

    x3 = x_ref[...]
    mod = mod_ref[...]
    h3 = _rms_modulate(x3, g2_ref[...], mod[:, :, 3 * d:4 * d], mod[:, :, 4 * d:5 * d])
    hb = h3.reshape(tm, d).astype(BF16)

    def conv_cols(c0):
        up = _dot(hb, wup_ref[:, c0:c0 + col_w]).reshape(nb, tl, col_w)
        ub_ref[:, pad - 2:pad, :] = carry_ref[:, :, c0:c0 + col_w]
        ub_ref[:, pad:pad + tl, :] = up
        carry_ref[:, :, c0:c0 + col_w] = ub_ref[:, pad + tl - 2:pad + tl, :]
        w = cw_ref[:, c0:c0 + col_w]
        out = cb_ref[:, c0:c0 + col_w][None] + ub_ref[:, pad - 2:pad - 2 + tl, :] * w[0:1][None]
        out = out + ub_ref[:, pad - 1:pad - 1 + tl, :] * w[1:2][None]
        out = out + up * w[2:3][None]
        return out.reshape(tm, col_w)

    acc = jnp.zeros((tm, d), F32)
    for c0 in range(0, d_ff, col_w):
        val = conv_cols(c0)
        gate = conv_cols(d_ff + c0)
        act = (gate * _sigmoid(gate) * val).astype(BF16)
        acc = acc + _dot(act, wdn_ref[c0:c0 + col_w, :])
    cs_ref[...] = carry_ref[...]
    xn = x3 + mod[:, :, 5 * d:6 * d] * acc.reshape(nb, tl, d)
    if final:
        xn = xn * lax.rsqrt(jnp.mean(xn * xn, axis=-1, keepdims=True) + RMS_EPS) * fg_ref[...][None]
    xo_ref[...] = xn


def _ffn_col_width(d_ff):
    n = d_ff // LANES
    for parts in range(2, n + 1):
        if n % parts == 0:
            return (n // parts) * LANES
    return d_ff


def _ffn(x, mod, g2, w_up, conv_w, conv_b, w_down, conv_prev, final_g, *, nb, tl, final):
    b, l, d = x.shape
    n_tiles = l // tl
    n_steps = (b // nb) * n_tiles
    c = w_up.shape[1]
    d_ff = w_down.shape[0]
    col_w = _ffn_col_width(d_ff)

    def tile_map(i):
        return (i // n_tiles, i % n_tiles, 0)

    def batch_map(i):
        return (i // n_tiles, 0, 0)

    const2 = lambda i: (0, 0)
    kern = functools.partial(_ffn_kernel, nb=nb, tl=tl, n_tiles=n_tiles, col_w=col_w, final=final)
    return pl.pallas_call(
        kern,
        grid=(n_steps,),
        in_specs=[
            pl.BlockSpec((nb, tl, d), tile_map),
            pl.BlockSpec((nb, 1, N_ADA * d), batch_map),
            _single((1, d), const2),
            _single(w_up.shape, const2),
            _single(conv_w.shape, const2),
            _single((1, c), const2),
            _single(w_down.shape, const2),
            pl.BlockSpec((nb, CONV_WIDTH - 1, c), batch_map),
            _single((1, d), const2),
        ],
        out_specs=[
            pl.BlockSpec((nb, tl, d), tile_map),
            pl.BlockSpec((nb, CONV_WIDTH - 1, c), batch_map),
        ],
        out_shape=[
            jax.ShapeDtypeStruct((b, l, d), F32),
            jax.ShapeDtypeStruct((b, CONV_WIDTH - 1, c), F32),
        ],
        scratch_shapes=[
            pltpu.VMEM((nb, SUBLANES + tl, col_w), F32),
            pltpu.VMEM((nb, CONV_WIDTH - 1, c), F32),
        ],
        compiler_params=pltpu.CompilerParams(dimension_semantics=("arbitrary",),
                                             vmem_limit_bytes=VMEM_LIMIT_BYTES),
        name="conv_ffn_final" if final else "conv_ffn",
    )(x, mod, g2, w_up, conv_w, conv_b, w_down, conv_prev, final_g)


def _rotary_tables(pos):
    half = ROPE_DIM // 2
    inv_freq = ROPE_THETA ** (-(jnp.arange(half, dtype=F32) * 2.0) / ROPE_DIM)
    ang = pos.astype(F32)[:, None] * inv_freq[None, :]
    cos = jnp.cos(ang)
    sin = jnp.sin(ang)
    n = pos.shape[0]
    rest = HEAD_DIM - ROPE_DIM
    cos_h = jnp.concatenate([cos, cos, jnp.ones((n, rest), F32)], axis=1)
    sina_h = jnp.concatenate([-sin, jnp.zeros((n, half + rest), F32)], axis=1)
    sinb_h = jnp.concatenate([jnp.zeros((n, half), F32), sin, jnp.zeros((n, rest), F32)], axis=1)
    reps = LANES // HEAD_DIM
    return tuple(jnp.tile(t, (1, reps))[None] for t in (cos_h, sina_h, sinb_h))


def kernel(x_prompt, x_sample, cache_k, cache_v, state_ssm_re, state_ssm_im, state_conv, c_prompt, c_sample, ada_w, ada_b, norm1_g, norm2_g, w_in, ssm_a_re, ssm_a_im, ssm_log_dt, ssm_b_re, ssm_b_im, ssm_c_re, ssm_c_im, ssm_d, w_glu, attn_sink, w_attn_o, w_out, ffn_w_up, ffn_conv_w, ffn_conv_b, ffn_w_down, final_g):
    bp, lp, d = x_prompt.shape
    bs, ls, _ = x_sample.shape
    depth = w_in.shape[0]
    groups, n_p = ssm_a_re.shape[1], ssm_a_re.shape[2]
    n_state = groups * n_p
    ssm_w = ssm_d.shape[-1]
    attn_w = N_HEADS * HEAD_DIM
    kv_w = N_KV_HEADS * HEAD_DIM
    c_ff = ffn_w_up.shape[-1]
    n_qkv = ssm_w + attn_w + 2 * kv_w
    tl_p = min(PROMPT_TILE, lp)
    t_steps_p = tl_p // SUBLANES
    n_pow = max(t_steps_p, ls)

    mod = _ada_modulation(jnp.concatenate([c_prompt, c_sample], axis=0), ada_w, ada_b)
    bbar_re, bbar_im, pw_re, pw_im = _discretise(ssm_a_re, ssm_a_im, ssm_log_dt, ssm_b_re, ssm_b_im, n_pow)
    pw_re = pw_re.reshape(depth, n_pow, n_state)
    pw_im = pw_im.reshape(depth, n_pow, n_state)
    gpb = MXU_DIM // SSM_GROUP

    tabs_p = _rotary_tables(jnp.arange(lp))
    tabs_s = _rotary_tables(PAST_LEN + jnp.arange(ls))
    zero_state = jnp.zeros((bp, 1, n_state), F32)
    zero_conv = jnp.zeros((bp, CONV_WIDTH - 1, c_ff), F32)
    zero_past = jnp.zeros((bp, SUBLANES, LANES), F32)

    xp, xs = x_prompt, x_sample
    outs_p = [[] for _ in range(5)]
    outs_s = [[] for _ in range(5)]
    for i in range(depth):
        mod_p = mod[i, :bp][:, None, :]
        mod_s = mod[i, bp:][:, None, :]
        w_mix = (
            norm1_g[i][None],
            w_in[i][:, :n_qkv].astype(BF16),
            w_in[i][:, n_qkv:].astype(BF16),
            _block_diag_in(bbar_re[i], gpb).astype(BF16),
            _block_diag_in(bbar_im[i], gpb).astype(BF16),
            _block_diag_out(ssm_c_re[i], gpb).astype(BF16),
            _block_diag_out(ssm_c_im[i], gpb).astype(BF16),
        )
        w_mix2 = (
            ssm_d[i][None],
            w_glu[i].astype(BF16),
            attn_sink[i],
            w_attn_o[i].astype(BF16),
            w_out[i].astype(BF16),
        )
        w_ffn = (
            norm2_g[i][None],
            ffn_w_up[i].astype(BF16),
            ffn_conv_w[i],
            ffn_conv_b[i][None],
            ffn_w_down[i].astype(BF16),
        )
        last = i == depth - 1

        xp, nk, nv, sr, si = _mixer(
            xp, mod_p, *w_mix, pw_re[i, :t_steps_p], pw_im[i, :t_steps_p], *w_mix2, *tabs_p,
            zero_state, zero_state, zero_past, zero_past, nb=1, tl=tl_p, cached=False)
        xp, cs = _ffn(xp, mod_p, *w_ffn, zero_conv, final_g[None], nb=1, tl=tl_p, final=last)
        for lst, val in zip(outs_p, (
                nk.reshape(bp, WINDOW, N_KV_HEADS, HEAD_DIM), nv.reshape(bp, WINDOW, N_KV_HEADS, HEAD_DIM),
                sr.reshape(bp, groups, n_p), si.reshape(bp, groups, n_p), cs)):
            lst.append(val)

        xs, nk, nv, sr, si = _mixer(
            xs, mod_s, *w_mix, pw_re[i, :ls], pw_im[i, :ls], *w_mix2, *tabs_s,
            state_ssm_re[i].reshape(1, bs, n_state), state_ssm_im[i].reshape(1, bs, n_state),
            cache_k[i].reshape(bs, WINDOW, kv_w), cache_v[i].reshape(bs, WINDOW, kv_w),
            nb=bs, tl=ls, cached=True)
        xs, cs = _ffn(xs, mod_s, *w_ffn, state_conv[i], final_g[None], nb=bs, tl=ls, final=last)
        for lst, val in zip(outs_s, (
                nk.reshape(bs, ls, N_KV_HEADS, HEAD_DIM), nv.reshape(bs, ls, N_KV_HEADS, HEAD_DIM),
                sr.reshape(bs, groups, n_p), si.reshape(bs, groups, n_p), cs)):
            lst.append(val)

    return (xp, xs) + tuple(jnp.stack(o, axis=0) for o in outs_p) + tuple(jnp.stack(o, axis=0) for o in outs_s)
```

```python
import functools
import math

import jax
import jax.numpy as jnp
from jax import lax
from jax.experimental import pallas as pl
from jax.experimental.pallas import tpu as pltpu

CHUNK = 64
SSM_GROUP = 16
SSM_STATE = 64
N_HEADS = 8
N_KV_HEADS = 2
HEAD_DIM = 64
WINDOW = 128
ROPE_DIM = HEAD_DIM // 4
ROPE_THETA = 500000.0
CONV_WIDTH = 3
N_ADA = 6
RMS_EPS = 1e-6
NEG_INF = -1e30
PAST_LEN = 2048

LANES = 128
SUBLANES = 8
MXU_DIM = 256
VMEM_LIMIT_BYTES = 58 * 1024 * 1024

PROMPT_TILE = 512
SCAN_LANES = 512
SCAN_SEGMENT = 16
GATE_COLS = 512
BF16 = jnp.bfloat16
F32 = jnp.float32


def _dot(a, b):
    return jnp.dot(a, b, preferred_element_type=F32)


def _dot_nt(a, b):
    return lax.dot_general(a, b, (((1,), (1,)), ((), ())), preferred_element_type=F32)


def _sigmoid(x):
    return 0.5 * jnp.tanh(0.5 * x) + 0.5


def _gelu_tanh(x):
    c = math.sqrt(2.0 / math.pi)
    return 0.5 * x * (1.0 + jnp.tanh(c * (x + 0.044715 * (x * x * x))))


def _rms_modulate(x3, g_row, shift3, scale3):
    y = x3 * lax.rsqrt(jnp.mean(x3 * x3, axis=-1, keepdims=True) + RMS_EPS)
    y = y * g_row[None]
    return y * (1.0 + scale3) + shift3


def _single(shape, index_map):
    return pl.BlockSpec(shape, index_map, pipeline_mode=pl.Buffered(1))


def _ada_kernel(c_ref, w_ref, b_ref, o_ref):
    c = c_ref[...]
    a = (c * _sigmoid(c)).astype(BF16)
    o_ref[0] = _dot(a, w_ref[0].astype(BF16)) + b_ref[0]


def _ada_modulation(c_all, ada_w, ada_b):
    depth, d, e = ada_w.shape
    nb = c_all.shape[0]
    col = d
    return pl.pallas_call(
        _ada_kernel,
        grid=(depth, e // col),
        in_specs=[
            pl.BlockSpec((nb, d), lambda l, j: (0, 0)),
            pl.BlockSpec((1, d, col), lambda l, j: (l, 0, j)),
            pl.BlockSpec((1, 1, col), lambda l, j: (l, 0, j)),
        ],
        out_specs=pl.BlockSpec((1, nb, col), lambda l, j: (l, 0, j)),
        out_shape=jax.ShapeDtypeStruct((depth, nb, e), F32),
        name="ada_modulation",
    )(c_all, ada_w, ada_b.reshape(depth, 1, e))


def _discretise_kernel(are_ref, aim_ref, ldt_ref, bre_ref, bim_ref,
                       bbr_ref, bbi_ref, pwr_ref, pwi_ref, *, n_pow):
    a_re = are_ref[0]
    a_im = aim_ref[0]
    dt = jnp.exp(ldt_ref[0])
    mag = jnp.exp(a_re * dt)
    abar_re = mag * jnp.cos(a_im * dt)
    abar_im = mag * jnp.sin(a_im * dt)
    nr = abar_re - 1.0
    ni = abar_im
    den = a_re * a_re + a_im * a_im
    fr = (nr * a_re + ni * a_im) / den
    fi = (ni * a_re - nr * a_im) / den
    b_re = bre_ref[0]
    b_im = bim_ref[0]
    bbr_ref[0] = fr[:, None, :] * b_re - fi[:, None, :] * b_im
    bbi_ref[0] = fr[:, None, :] * b_im + fi[:, None, :] * b_re
    pr, pi = abar_re, abar_im
    pwr_ref[0, 0] = pr
    pwi_ref[0, 0] = pi
    for t in range(1, n_pow):
        pr, pi = abar_re * pr - abar_im * pi, abar_re * pi + abar_im * pr
        pwr_ref[0, t] = pr
        pwi_ref[0, t] = pi


def _discretise(a_re, a_im, log_dt, b_re, b_im, n_pow):
    depth, g, p = a_re.shape
    gc = b_re.shape[-1]
    b_re_t = jnp.swapaxes(b_re, 2, 3)
    b_im_t = jnp.swapaxes(b_im, 2, 3)
    spec_a = pl.BlockSpec((1, g, p), lambda l: (l, 0, 0))
    spec_b = pl.BlockSpec((1, g, gc, p), lambda l: (l, 0, 0, 0))
    spec_p = pl.BlockSpec((1, n_pow, g, p), lambda l: (l, 0, 0, 0))
    return pl.pallas_call(
        functools.partial(_discretise_kernel, n_pow=n_pow),
        grid=(depth,),
        in_specs=[spec_a, spec_a, pl.BlockSpec((1, g, 1), lambda l: (l, 0, 0)), spec_b, spec_b],
        out_specs=[spec_b, spec_b, spec_p, spec_p],
        out_shape=[jax.ShapeDtypeStruct((depth, g, gc, p), F32)] * 2
        + [jax.ShapeDtypeStruct((depth, n_pow, g, p), F32)] * 2,
        name="s5_discretise",
    )(a_re, a_im, log_dt.reshape(depth, g, 1), b_re_t, b_im_t)


def _block_diag_in(bbar, groups_per_block):
    g, gc, p = bbar.shape
    nb = g // groups_per_block
    eye = jnp.eye(groups_per_block, dtype=bbar.dtype)
    b = bbar.reshape(nb, groups_per_block, gc, p)
    out = b[:, :, :, None, :] * eye[None, :, None, :, None]
    return out.reshape(nb, groups_per_block * gc, groups_per_block * p)


def _block_diag_out(c, groups_per_block):
    g, gc, p = c.shape
    nb = g // groups_per_block
    eye = jnp.eye(groups_per_block, dtype=c.dtype)
    ct = jnp.swapaxes(c, 1, 2).reshape(nb, groups_per_block, p, gc)
    out = ct[:, :, :, None, :] * eye[None, :, None, :, None]
    return out.reshape(nb, groups_per_block * p, groups_per_block * gc)


def _mixer_kernel(x_ref, mod_ref, g1_ref, win_ref, wgate_ref, bre_ref, bim_ref, cre_ref, cim_ref,
                  pwr_ref, pwi_ref, dskip_ref, wglu_ref, sink_ref, wao_ref, wout_ref,
                  cos_ref, sina_ref, sinb_ref, cq_ref, sq_ref, gk_ref, gv_ref, hmask_ref,
                  s0r_ref, s0i_ref, pk_ref, pv_ref,
                  xo_ref, ko_ref, vo_ref, sro_ref, sio_ref,
                  h_ref, u_ref, up_ref, xr_ref, xi_ref, yp_ref, sinr_ref, sini_ref,
                  carr_ref, cari_ref, kb_ref, vb_ref, kd_ref, vd_ref, ao_ref,
                  *, nb, tl, nch, t_steps, n_tiles, cached):
    tm = nb * tl
    d = x_ref.shape[-1]
    ssm_w = up_ref.shape[-1]
    n_state = xr_ref.shape[-1]
    attn_w = N_HEADS * HEAD_DIM
    kv_w = N_KV_HEADS * HEAD_DIM
    tile = pl.program_id(0) % n_tiles
    first = tile == 0

    x3 = x_ref[...]
    mod = mod_ref[...]
    h3 = _rms_modulate(x3, g1_ref[...], mod[:, :, 0:d], mod[:, :, d:2 * d])
    hb = h3.reshape(tm, d).astype(BF16)
    h_ref[...] = hb
    proj = _dot(hb, win_ref[...])
    u = proj[:, 0:ssm_w]
    for c in range(ssm_w // LANES):
        u_ref[c] = u[:, c * LANES:(c + 1) * LANES]
    q = proj[:, ssm_w:ssm_w + attn_w]
    k = proj[:, ssm_w + attn_w:ssm_w + attn_w + kv_w]
    v = proj[:, ssm_w + attn_w + kv_w:ssm_w + attn_w + 2 * kv_w]

    fillers = []

    cos3 = cos_ref[...]
    sina3 = sina_ref[...]
    sinb3 = sinb_ref[...]
    k_rolled_down = pltpu.roll(k, LANES - ROPE_DIM // 2, axis=1).reshape(nb, tl, LANES)
    k_rolled_up = pltpu.roll(k, ROPE_DIM // 2, axis=1).reshape(nb, tl, LANES)
    k3 = k.reshape(nb, tl, LANES) * cos3 + k_rolled_down * sina3 + k_rolled_up * sinb3
    v3 = v.reshape(nb, tl, LANES)
    kr = WINDOW if not cached else tl
    ko_ref[...] = k3[:, tl - kr:, :]
    vo_ref[...] = v3[:, tl - kr:, :]
    if not cached:
        @pl.when(first)
        def _():
            kb_ref[0:WINDOW, :] = jnp.zeros((WINDOW, LANES), F32)
            vb_ref[0:WINDOW, :] = jnp.zeros((WINDOW, LANES), F32)

        kb_ref[WINDOW:WINDOW + tm, :] = k3.reshape(tm, LANES)
        vb_ref[WINDOW:WINDOW + tm, :] = v
        kall = kb_ref[...]
        vall = vb_ref[...]
        band = WINDOW + CHUNK
        n_rows = CHUNK
        n_blocks = tm // CHUNK
    else:
        kb_ref[:, 0:WINDOW, :] = pk_ref[...]
        kb_ref[:, WINDOW:WINDOW + tl, :] = k3
        vb_ref[:, 0:WINDOW, :] = pv_ref[...]
        vb_ref[:, WINDOW:WINDOW + tl, :] = v3
        band = WINDOW + tl
        n_rows = tl
        n_blocks = nb
        kall = kb_ref[...].reshape(nb * band, LANES)
        vall = vb_ref[...].reshape(nb * band, LANES)
    group_w = 2 * LANES
    k_spread = _dot(kall.astype(BF16), gk_ref[...]).astype(BF16)
    v_spread = _dot(vall.astype(BF16), gv_ref[...]).astype(BF16)
    for j in range(N_KV_HEADS):
        kd_ref[j] = k_spread[:, j * group_w:(j + 1) * group_w]
        vd_ref[j, :, 0:LANES] = v_spread[:, j * LANES:(j + 1) * LANES]
        vd_ref[j, :, LANES:group_w] = jnp.ones((kall.shape[0], LANES), BF16)
    if not cached:
        kb_ref[0:WINDOW, :] = kb_ref[tm:tm + WINDOW, :]
        vb_ref[0:WINDOW, :] = vb_ref[tm:tm + WINDOW, :]

    scale = HEAD_DIM ** -0.5
    cq3 = cq_ref[...]
    sq3 = sq_ref[...]
    q_groups = []
    for j in range(N_KV_HEADS):
        c0 = q[:, j * group_w:j * group_w + LANES].reshape(nb, tl, LANES)
        c1 = q[:, j * group_w + LANES:(j + 1) * group_w].reshape(nb, tl, LANES)
        r0c = (c0 * cq3 - c1 * sq3).reshape(tm, LANES)
        r1c = (c1 * cq3 + c0 * sq3).reshape(tm, LANES)
        q_groups.append(jnp.concatenate([r0c, r1c], axis=1) * scale)
    q_per_kv = N_HEADS // N_KV_HEADS
    lane = lax.broadcasted_iota(jnp.int32, (1, LANES), 1)
    low_half = lane < HEAD_DIM
    key_idx = lax.broadcasted_iota(jnp.int32, (1, band), 1)

    def attend(blk, j):
        r0 = blk * n_rows
        b0 = blk * (CHUNK if not cached else band)
        qg = q_groups[j][r0:r0 + n_rows]
        lhs = jnp.concatenate([qg * hmask_ref[g:g + 1, :] for g in range(q_per_kv)],
                              axis=0).astype(BF16)
        s = _dot_nt(lhs, kd_ref[j, b0:b0 + band, :])
        if not cached and blk < WINDOW // CHUNK:
            first_valid = jnp.where(first, (WINDOW // CHUNK - blk) * CHUNK, 0)
            s = jnp.where(key_idx >= first_valid, s, NEG_INF)
        sink = jnp.concatenate(
            [jnp.full((n_rows, 1), sink_ref[j * q_per_kv + g], F32) for g in range(q_per_kv)], axis=0)
        m = jnp.maximum(jnp.max(s, axis=-1, keepdims=True), sink)
        pexp = jnp.exp(s - m)
        o2 = _dot(pexp.astype(BF16), vd_ref[j, b0:b0 + band, :])
        o = o2[:, 0:LANES] / (o2[:, LANES:group_w] + jnp.exp(sink - m))
        for p in range(q_per_kv // 2):
            lo_rows = o[(2 * p) * n_rows:(2 * p + 1) * n_rows]
            hi_rows = o[(2 * p + 1) * n_rows:(2 * p + 2) * n_rows]
            col = (j * (q_per_kv // 2) + p) * LANES
            ao_ref[r0:r0 + n_rows, col:col + LANES] = jnp.where(low_half, lo_rows, hi_rows).astype(BF16)

    for blk in range(n_blocks):
        for j in range(N_KV_HEADS):
            fillers.append(functools.partial(attend, blk, j))

    n_gate_blocks = wgate_ref.shape[1] // GATE_COLS
    gate_blocks = [None] * n_gate_blocks

    def project_gates(gi):
        gate_blocks[gi] = _dot(h_ref[...], wgate_ref[:, gi * GATE_COLS:(gi + 1) * GATE_COLS])

    for gi in range(n_gate_blocks):
        fillers.append(functools.partial(project_gates, gi))
    fillers.reverse()

    def run_fillers(count):
        for _ in range(min(count, len(fillers))):
            fillers.pop()()

    for t in range(t_steps):
        for c in range(ssm_w // LANES):
            up_ref[t * nch:(t + 1) * nch, c * LANES:(c + 1) * LANES] = u_ref[c, pl.ds(t, nch, stride=t_steps), :]
    n_in_blocks = bre_ref.shape[0]
    kin = ssm_w // n_in_blocks
    nin = n_state // n_in_blocks
    for j in range(n_in_blocks):
        ub = up_ref[:, j * kin:(j + 1) * kin].astype(BF16)
        xr_ref[:, j * nin:(j + 1) * nin] = _dot(ub, bre_ref[j])
        xi_ref[:, j * nin:(j + 1) * nin] = _dot(ub, bim_ref[j])

    n_scans = (1 if cached else 2) * (n_state // SCAN_LANES)
    n_segments = n_scans * (t_steps // SCAN_SEGMENT)
    fillers_per_segment = -(-len(fillers) // n_segments)

    def scan_block(lo, s_r, s_i, store):
        a_r = jnp.broadcast_to(pwr_ref[0:1, lo:lo + SCAN_LANES], (nch, SCAN_LANES))
        a_i = jnp.broadcast_to(pwi_ref[0:1, lo:lo + SCAN_LANES], (nch, SCAN_LANES))
        for t in range(t_steps):
            rows = slice(t * nch, (t + 1) * nch)
            in_r = xr_ref[rows, lo:lo + SCAN_LANES]
            in_i = xi_ref[rows, lo:lo + SCAN_LANES]
            s_r, s_i = a_r * s_r - a_i * s_i + in_r, a_r * s_i + a_i * s_r + in_i
            if store:
                xr_ref[rows, lo:lo + SCAN_LANES] = s_r
                xi_ref[rows, lo:lo + SCAN_LANES] = s_i
            if (t + 1) % SCAN_SEGMENT == 0:
                run_fillers(fillers_per_segment)
        return s_r, s_i

    if not cached:
        @pl.when(first)
        def _():
            carr_ref[...] = s0r_ref[0]
            cari_ref[...] = s0i_ref[0]

        zero = jnp.zeros((nch, SCAN_LANES), F32)
        for lo in range(0, n_state, SCAN_LANES):
            f_r, f_i = scan_block(lo, zero, zero, store=False)
            at_r = pwr_ref[t_steps - 1:t_steps, lo:lo + SCAN_LANES]
            at_i = pwi_ref[t_steps - 1:t_steps, lo:lo + SCAN_LANES]
            c_r = carr_ref[:, lo:lo + SCAN_LANES]
            c_i = cari_ref[:, lo:lo + SCAN_LANES]
            for ch in range(nch):
                sinr_ref[ch:ch + 1, lo:lo + SCAN_LANES] = c_r
                sini_ref[ch:ch + 1, lo:lo + SCAN_LANES] = c_i
                c_r, c_i = (at_r * c_r - at_i * c_i + f_r[ch:ch + 1],
                            at_r * c_i + at_i * c_r + f_i[ch:ch + 1])
            carr_ref[:, lo:lo + SCAN_LANES] = c_r
            cari_ref[:, lo:lo + SCAN_LANES] = c_i
        sro_ref[0] = carr_ref[...]
        sio_ref[0] = cari_ref[...]
    else:
        sinr_ref[...] = s0r_ref[0]
        sini_ref[...] = s0i_ref[0]

    for lo in range(0, n_state, SCAN_LANES):
        f_r, f_i = scan_block(lo, sinr_ref[:, lo:lo + SCAN_LANES], sini_ref[:, lo:lo + SCAN_LANES],
                              store=True)
        if cached:
            sro_ref[0, :, lo:lo + SCAN_LANES] = f_r
            sio_ref[0, :, lo:lo + SCAN_LANES] = f_i
    run_fillers(len(fillers))

    n_out_blocks = cre_ref.shape[0]
    kout = n_state // n_out_blocks
    nout = ssm_w // n_out_blocks
    for j in range(n_out_blocks):
        sr = xr_ref[:, j * kout:(j + 1) * kout].astype(BF16)
        si = xi_ref[:, j * kout:(j + 1) * kout].astype(BF16)
        yb = _dot(sr, cre_ref[j]) - _dot(si, cim_ref[j])
        for c in range(nout // LANES):
            yp_ref[j * (nout // LANES) + c] = yb[:, c * LANES:(c + 1) * LANES]
    y = jnp.concatenate(
        [jnp.concatenate([yp_ref[c, pl.ds(ch, t_steps, stride=nch), :] for ch in range(nch)], axis=0)
         for c in range(ssm_w // LANES)], axis=1)
    y = y + dskip_ref[...] * u
    glu = _dot(_gelu_tanh(y).astype(BF16), wglu_ref[...])
    branch_a = glu[:, 0:d] * _sigmoid(glu[:, d:2 * d])
    branch_b = _dot(ao_ref[...], wao_ref[...])

    gates = jnp.concatenate(gate_blocks, axis=1)
    merged = _sigmoid(gates[:, 0:d]) * branch_a + _sigmoid(gates[:, d:2 * d]) * branch_b
    out = _dot(merged.astype(BF16), wout_ref[...])
    xo_ref[...] = x3 + mod[:, :, 2 * d:3 * d] * out.reshape(nb, tl, d)


def _mixer(x, mod, g1, w_in, w_gate, b_re, b_im, c_re, c_im, pw_re, pw_im, d_skip, w_glu, sink,
           w_ao, w_out, tables, gather_k, gather_v, head_mask, s0_re, s0_im, past_k, past_v,
           *, nb, tl, cached):
    b, l, d = x.shape
    n_tiles = l // tl
    n_steps = (b // nb) * n_tiles
    tm = nb * tl
    ssm_w = d_skip.shape[-1]
    n_state = pw_re.shape[-1]
    attn_w = N_HEADS * HEAD_DIM
    if cached:
        nch, t_steps = nb, tl
    else:
        nch, t_steps = SUBLANES, tl // SUBLANES
    kr = tl if cached else WINDOW
    band_rows = (nb, WINDOW + tl, LANES) if cached else (WINDOW + tm, LANES)
    dup_rows = nb * (WINDOW + tl) if cached else WINDOW + tm

    def tile_map(i):
        return (i // n_tiles, i % n_tiles, 0)

    def batch_map(i):
        return (i // n_tiles, 0, 0)

    const2 = lambda i: (0, 0)
    const3 = lambda i: (0, 0, 0)
    pos_spec = pl.BlockSpec((1, tl, LANES), lambda i: (0, i % n_tiles, 0))
    in_specs = [
        pl.BlockSpec((nb, tl, d), tile_map),
        pl.BlockSpec((nb, 1, N_ADA * d), batch_map),
        _single((1, d), const2),
        _single(w_in.shape, const2),
        _single(w_gate.shape, const2),
        _single(b_re.shape, const3),
        _single(b_im.shape, const3),
        _single(c_re.shape, const3),
        _single(c_im.shape, const3),
        _single(pw_re.shape, const2),
        _single(pw_im.shape, const2),
        _single((1, ssm_w), const2),
        _single(w_glu.shape, const2),
        pl.BlockSpec(memory_space=pltpu.SMEM),
        _single(w_ao.shape, const2),
        _single(w_out.shape, const2),
        pos_spec, pos_spec, pos_spec, pos_spec, pos_spec,
        _single(gather_k.shape, const2),
        _single(gather_v.shape, const2),
        _single(head_mask.shape, const2),
        pl.BlockSpec((1, s0_re.shape[1], n_state), batch_map),
        pl.BlockSpec((1, s0_im.shape[1], n_state), batch_map),
        pl.BlockSpec((nb,) + past_k.shape[1:], batch_map),
        pl.BlockSpec((nb,) + past_v.shape[1:], batch_map),
    ]
    out_specs = [
        pl.BlockSpec((nb, tl, d), tile_map),
        pl.BlockSpec((nb, kr, LANES), batch_map),
        pl.BlockSpec((nb, kr, LANES), batch_map),
        pl.BlockSpec((1, s0_re.shape[1], n_state), batch_map),
        pl.BlockSpec((1, s0_re.shape[1], n_state), batch_map),
    ]
    out_shape = [
        jax.ShapeDtypeStruct((b, l, d), F32),
        jax.ShapeDtypeStruct((b, kr, LANES), F32),
        jax.ShapeDtypeStruct((b, kr, LANES), F32),
        jax.ShapeDtypeStruct(s0_re.shape, F32),
        jax.ShapeDtypeStruct(s0_im.shape, F32),
    ]
    scratch = [
        pltpu.VMEM((tm, d), BF16),
        pltpu.VMEM((ssm_w // LANES, tm, LANES), F32),
        pltpu.VMEM((tm, ssm_w), F32),
        pltpu.VMEM((tm, n_state), F32),
        pltpu.VMEM((tm, n_state), F32),
        pltpu.VMEM((ssm_w // LANES, tm, LANES), F32),
        pltpu.VMEM((nch, n_state), F32),
        pltpu.VMEM((nch, n_state), F32),
        pltpu.VMEM((1, n_state), F32),
        pltpu.VMEM((1, n_state), F32),
        pltpu.VMEM(band_rows, F32),
        pltpu.VMEM(band_rows, F32),
        pltpu.VMEM((N_KV_HEADS, dup_rows, 2 * LANES), BF16),
        pltpu.VMEM((N_KV_HEADS, dup_rows, 2 * LANES), BF16),
        pltpu.VMEM((tm, attn_w), BF16),
    ]
    kern = functools.partial(_mixer_kernel, nb=nb, tl=tl, nch=nch, t_steps=t_steps,
                             n_tiles=n_tiles, cached=cached)
    return pl.pallas_call(
        kern,
        grid=(n_steps,),
        in_specs=in_specs,
        out_specs=out_specs,
        out_shape=out_shape,
        scratch_shapes=scratch,
        compiler_params=pltpu.CompilerParams(dimension_semantics=("arbitrary",),
                                             vmem_limit_bytes=VMEM_LIMIT_BYTES),
        name="mixer_cached" if cached else "mixer_banded",
    )(x, mod, g1, w_in, w_gate, b_re, b_im, c_re, c_im, pw_re, pw_im, d_skip, w_glu, sink,
      w_ao, w_out, *tables, gather_k, gather_v, head_mask, s0_re, s0_im, past_k, past_v)


def _ffn_kernel(x_ref, mod_ref, g2_ref, wup_ref, cw_ref, cb_ref, wdn_ref, prev_ref, fg_ref,
                xo_ref, cs_ref, ub_ref, carry_ref, *, nb, tl, n_tiles, col_w, final):
    tm = nb * tl
    d = x_ref.shape[-1]
    d_ff = wdn_ref.shape[0]
    pad = SUBLANES
    first = pl.program_id(0) % n_tiles == 0

    @pl.when(first)
    def _():
        carry_ref[...] = prev_ref[...]

    x3 = x_ref[...]
    mod = mod_ref[...]
    h3 = _rms_modulate(x3, g2_ref[...], mod[:, :, 3 * d:4 * d], mod[:, :, 4 * d:5 * d])
    hb = h3.reshape(tm, d).astype(BF16)

    def conv_cols(c0):
        up = _dot(hb, wup_ref[:, c0:c0 + col_w]).reshape(nb, tl, col_w)
        ub_ref[:, pad - 2:pad, :] = carry_ref[:, :, c0:c0 + col_w]
        ub_ref[:, pad:pad + tl, :] = up
        carry_ref[:, :, c0:c0 + col_w] = ub_ref[:, pad + tl - 2:pad + tl, :]
        w = cw_ref[:, c0:c0 + col_w]
        out = cb_ref[:, c0:c0 + col_w][None] + ub_ref[:, pad - 2:pad - 2 + tl, :] * w[0:1][None]
        out = out + ub_ref[:, pad - 1:pad - 1 + tl, :] * w[1:2][None]
        out = out + up * w[2:3][None]
        return out.reshape(tm, col_w)

    acc = jnp.zeros((tm, d), F32)
    for c0 in range(0, d_ff, col_w):
        val = conv_cols(c0)
        gate = conv_cols(d_ff + c0)
        act = (gate * _sigmoid(gate) * val).astype(BF16)
        acc = acc + _dot(act, wdn_ref[c0:c0 + col_w, :])
    cs_ref[...] = carry_ref[...]
    xn = x3 + mod[:, :, 5 * d:6 * d] * acc.reshape(nb, tl, d)
    if final:
        xn = xn * lax.rsqrt(jnp.mean(xn * xn, axis=-1, keepdims=True) + RMS_EPS) * fg_ref[...][None]
    xo_ref[...] = xn


def _ffn_col_width(d_ff):
    n = d_ff // LANES
    for parts in range(2, n + 1):
        if n % parts == 0:
            return (n // parts) * LANES
    return d_ff


def _ffn(x, mod, g2, w_up, conv_w, conv_b, w_down, conv_prev, final_g, *, nb, tl, final):
    b, l, d = x.shape
    n_tiles = l // tl
    n_steps = (b // nb) * n_tiles
    c = w_up.shape[1]
    d_ff = w_down.shape[0]
    col_w = _ffn_col_width(d_ff)

    def tile_map(i):
        return (i // n_tiles, i % n_tiles, 0)

    def batch_map(i):
        return (i // n_tiles, 0, 0)

    const2 = lambda i: (0, 0)
    kern = functools.partial(_ffn_kernel, nb=nb, tl=tl, n_tiles=n_tiles, col_w=col_w, final=final)
    return pl.pallas_call(
        kern,
        grid=(n_steps,),
        in_specs=[
            pl.BlockSpec((nb, tl, d), tile_map),
            pl.BlockSpec((nb, 1, N_ADA * d), batch_map),
            _single((1, d), const2),
            _single(w_up.shape, const2),
            _single(conv_w.shape, const2),
            _single((1, c), const2),
            _single(w_down.shape, const2),
            pl.BlockSpec((nb, CONV_WIDTH - 1, c), batch_map),
            _single((1, d), const2),
        ],
        out_specs=[
            pl.BlockSpec((nb, tl, d), tile_map),
            pl.BlockSpec((nb, CONV_WIDTH - 1, c), batch_map),
        ],
        out_shape=[
            jax.ShapeDtypeStruct((b, l, d), F32),
            jax.ShapeDtypeStruct((b, CONV_WIDTH - 1, c), F32),
        ],
        scratch_shapes=[
            pltpu.VMEM((nb, SUBLANES + tl, col_w), F32),
            pltpu.VMEM((nb, CONV_WIDTH - 1, c), F32),
        ],
        compiler_params=pltpu.CompilerParams(dimension_semantics=("arbitrary",),
                                             vmem_limit_bytes=VMEM_LIMIT_BYTES),
        name="conv_ffn_final" if final else "conv_ffn",
    )(x, mod, g2, w_up, conv_w, conv_b, w_down, conv_prev, final_g)


def _rotary_tables(pos):
    half = ROPE_DIM // 2
    inv_freq = ROPE_THETA ** (-(jnp.arange(half, dtype=F32) * 2.0) / ROPE_DIM)
    ang = pos.astype(F32)[:, None] * inv_freq[None, :]
    cos = jnp.cos(ang)
    sin = jnp.sin(ang)
    n = pos.shape[0]
    rest = HEAD_DIM - ROPE_DIM
    cos_h = jnp.concatenate([cos, cos, jnp.ones((n, rest), F32)], axis=1)
    sina_h = jnp.concatenate([-sin, jnp.zeros((n, half + rest), F32)], axis=1)
    sinb_h = jnp.concatenate([jnp.zeros((n, half), F32), sin, jnp.zeros((n, rest), F32)], axis=1)
    reps = LANES // HEAD_DIM
    key_tabs = tuple(jnp.tile(t, (1, reps))[None] for t in (cos_h, sina_h, sinb_h))
    q_per_kv = N_HEADS // N_KV_HEADS
    n_rot = q_per_kv * half
    cos_q = jnp.concatenate([jnp.tile(cos, (1, q_per_kv)), jnp.ones((n, LANES - n_rot), F32)], axis=1)
    sin_q = jnp.concatenate([jnp.tile(sin, (1, q_per_kv)), jnp.zeros((n, LANES - n_rot), F32)], axis=1)
    return key_tabs + (cos_q[None], sin_q[None])


def _group_layout():
    q_per_kv = N_HEADS // N_KV_HEADS
    half = ROPE_DIM // 2
    rest = HEAD_DIM - ROPE_DIM
    n_rot = q_per_kv * half
    slots, dims = [], []
    for col in range(2):
        for lane in range(LANES):
            if lane < n_rot:
                slots.append(lane // half)
                dims.append(lane % half + half * col)
            else:
                r = lane - n_rot
                slots.append(col * (q_per_kv // 2) + r // rest)
                dims.append(ROPE_DIM + r % rest)
    return slots, dims


def _attention_constants():
    q_per_kv = N_HEADS // N_KV_HEADS
    kv_w = N_KV_HEADS * HEAD_DIM
    slots, dims = _group_layout()
    group_w = len(slots)
    q_perm = [(j * q_per_kv + slots[n]) * HEAD_DIM + dims[n] for j in range(N_KV_HEADS) for n in range(group_w)]
    dims_a = jnp.asarray(dims)
    slots_a = jnp.asarray(slots)
    rows = jnp.arange(kv_w)[:, None]
    gather_k = jnp.concatenate(
        [(rows == j * HEAD_DIM + dims_a[None, :]) for j in range(N_KV_HEADS)], axis=1).astype(BF16)
    pair_dim = jnp.arange(LANES) % HEAD_DIM
    gather_v = jnp.concatenate(
        [(rows == j * HEAD_DIM + pair_dim[None, :]) for j in range(N_KV_HEADS)], axis=1).astype(BF16)
    head_mask = (slots_a[None, :] == jnp.arange(q_per_kv)[:, None]).astype(F32)
    return jnp.asarray(q_perm), gather_k, gather_v, head_mask


def kernel(x_prompt, x_sample, cache_k, cache_v, state_ssm_re, state_ssm_im, state_conv, c_prompt, c_sample, ada_w, ada_b, norm1_g, norm2_g, w_in, ssm_a_re, ssm_a_im, ssm_log_dt, ssm_b_re, ssm_b_im, ssm_c_re, ssm_c_im, ssm_d, w_glu, attn_sink, w_attn_o, w_out, ffn_w_up, ffn_conv_w, ffn_conv_b, ffn_w_down, final_g):
    bp, lp, d = x_prompt.shape
    bs, ls, _ = x_sample.shape
    depth = w_in.shape[0]
    groups, n_p = ssm_a_re.shape[1], ssm_a_re.shape[2]
    n_state = groups * n_p
    ssm_w = ssm_d.shape[-1]
    attn_w = N_HEADS * HEAD_DIM
    kv_w = N_KV_HEADS * HEAD_DIM
    c_ff = ffn_w_up.shape[-1]
    n_qkv = ssm_w + attn_w + 2 * kv_w
    tl_p = min(PROMPT_TILE, lp)
    t_steps_p = tl_p // SUBLANES
    n_pow = max(t_steps_p, ls)

    mod = _ada_modulation(jnp.concatenate([c_prompt, c_sample], axis=0), ada_w, ada_b)
    bbar_re, bbar_im, pw_re, pw_im = _discretise(ssm_a_re, ssm_a_im, ssm_log_dt, ssm_b_re, ssm_b_im, n_pow)
    pw_re = pw_re.reshape(depth, n_pow, n_state)
    pw_im = pw_im.reshape(depth, n_pow, n_state)
    gpb = MXU_DIM // SSM_GROUP

    tabs_p = _rotary_tables(jnp.arange(lp))
    tabs_s = _rotary_tables(PAST_LEN + jnp.arange(ls))
    q_perm, gather_k, gather_v, head_mask = _attention_constants()
    attn_consts = (gather_k, gather_v, head_mask)
    zero_state = jnp.zeros((bp, 1, n_state), F32)
    zero_conv = jnp.zeros((bp, CONV_WIDTH - 1, c_ff), F32)
    zero_past = jnp.zeros((bp, SUBLANES, LANES), F32)

    xp, xs = x_prompt, x_sample
    outs_p = [[] for _ in range(5)]
    outs_s = [[] for _ in range(5)]
    for i in range(depth):
        mod_p = mod[i, :bp][:, None, :]
        mod_s = mod[i, bp:][:, None, :]
        w_mix = (
            norm1_g[i][None],
            jnp.concatenate([w_in[i][:, :ssm_w], w_in[i][:, ssm_w + q_perm],
                             w_in[i][:, ssm_w + attn_w:n_qkv]], axis=1).astype(BF16),
            w_in[i][:, n_qkv:].astype(BF16),
            _block_diag_in(bbar_re[i], gpb).astype(BF16),
            _block_diag_in(bbar_im[i], gpb).astype(BF16),
            _block_diag_out(ssm_c_re[i], gpb).astype(BF16),
            _block_diag_out(ssm_c_im[i], gpb).astype(BF16),
        )
        w_mix2 = (
            ssm_d[i][None],
            w_glu[i].astype(BF16),
            attn_sink[i],
            w_attn_o[i].astype(BF16),
            w_out[i].astype(BF16),
        )
        w_ffn = (
            norm2_g[i][None],
            ffn_w_up[i].astype(BF16),
            ffn_conv_w[i],
            ffn_conv_b[i][None],
            ffn_w_down[i].astype(BF16),
        )
        last = i == depth - 1

        xp, nk, nv, sr, si = _mixer(
            xp, mod_p, *w_mix, pw_re[i, :t_steps_p], pw_im[i, :t_steps_p], *w_mix2, tabs_p, *attn_consts,
            zero_state, zero_state, zero_past, zero_past, nb=1, tl=tl_p, cached=False)
        xp, cs = _ffn(xp, mod_p, *w_ffn, zero_conv, final_g[None], nb=1, tl=tl_p, final=last)
        for lst, val in zip(outs_p, (
                nk.reshape(bp, WINDOW, N_KV_HEADS, HEAD_DIM), nv.reshape(bp, WINDOW, N_KV_HEADS, HEAD_DIM),
                sr.reshape(bp, groups, n_p), si.reshape(bp, groups, n_p), cs)):
            lst.append(val)

        xs, nk, nv, sr, si = _mixer(
            xs, mod_s, *w_mix, pw_re[i, :ls], pw_im[i, :ls], *w_mix2, tabs_s, *attn_consts,
            state_ssm_re[i].reshape(1, bs, n_state), state_ssm_im[i].reshape(1, bs, n_state),
            cache_k[i].reshape(bs, WINDOW, kv_w), cache_v[i].reshape(bs, WINDOW, kv_w),
            nb=bs, tl=ls, cached=True)
        xs, cs = _ffn(xs, mod_s, *w_ffn, state_conv[i], final_g[None], nb=bs, tl=ls, final=last)
        for lst, val in zip(outs_s, (
                nk.reshape(bs, ls, N_KV_HEADS, HEAD_DIM), nv.reshape(bs, ls, N_KV_HEADS, HEAD_DIM),
                sr.reshape(bs, groups, n_p), si.reshape(bs, groups, n_p), cs)):
            lst.append(val)

    return (xp, xs) + tuple(jnp.stack(o, axis=0) for o in outs_p) + tuple(jnp.stack(o, axis=0) for o in outs_s)
```

```python
import functools
import math
import types

import jax
import jax.numpy as jnp
from jax import lax
from jax.experimental import pallas as pl
from jax.experimental.pallas import tpu as pltpu

CHUNK = 64
SSM_GROUP = 16
SSM_STATE = 64
N_HEADS = 8
N_KV_HEADS = 2
HEAD_DIM = 64
WINDOW = 128
ROPE_DIM = HEAD_DIM // 4
ROPE_THETA = 500000.0
CONV_WIDTH = 3
N_ADA = 6
RMS_EPS = 1e-6
NEG_INF = -1e30
PAST_LEN = 2048

LANES = 128
SUBLANES = 8
MXU_DIM = 256
VMEM_LIMIT_BYTES = 58 * 1024 * 1024

PROMPT_TILE = 512
SCAN_LANES = 512
BF16 = jnp.bfloat16
F32 = jnp.float32


def _dot(a, b):
    return jnp.dot(a, b, preferred_element_type=F32)


def _dot_nt(a, b):
    return lax.dot_general(a, b, (((1,), (1,)), ((), ())), preferred_element_type=F32)


def _sigmoid(x):
    return 0.5 * jnp.tanh(0.5 * x) + 0.5


def _gelu_tanh(x):
    c = math.sqrt(2.0 / math.pi)
    return 0.5 * x * (1.0 + jnp.tanh(c * (x + 0.044715 * (x * x * x))))


def _rms_modulate(x3, g_row, shift3, scale3):
    y = x3 * lax.rsqrt(jnp.mean(x3 * x3, axis=-1, keepdims=True) + RMS_EPS)
    y = y * g_row[None]
    return y * (1.0 + scale3) + shift3


def _named(names, refs):
    assert len(names) == len(refs), (len(names), len(refs))
    return types.SimpleNamespace(**dict(zip(names, refs)))


def _layer_block(a, layer):
    zeros = (0,) * (a.ndim - 1)
    return pl.BlockSpec((None,) + a.shape[1:], lambda i: (layer,) + zeros, pipeline_mode=pl.Buffered(1))


def _whole(a):
    zeros = (0,) * a.ndim
    return pl.BlockSpec(a.shape, lambda i: zeros, pipeline_mode=pl.Buffered(1))


def _ada_kernel(c_ref, w_ref, b_ref, o_ref):
    c = c_ref[...]
    a = (c * _sigmoid(c)).astype(BF16)
    o_ref[0] = _dot(a, w_ref[0].astype(BF16)) + b_ref[0]


def _ada_modulation(c_all, ada_w, ada_b):
    depth, d, e = ada_w.shape
    nb = c_all.shape[0]
    col = d
    return pl.pallas_call(
        _ada_kernel,
        grid=(depth, e // col),
        in_specs=[
            pl.BlockSpec((nb, d), lambda l, j: (0, 0)),
            pl.BlockSpec((1, d, col), lambda l, j: (l, 0, j)),
            pl.BlockSpec((1, 1, col), lambda l, j: (l, 0, j)),
        ],
        out_specs=pl.BlockSpec((1, nb, col), lambda l, j: (l, 0, j)),
        out_shape=jax.ShapeDtypeStruct((depth, nb, e), F32),
        name="ada_modulation",
    )(c_all, ada_w, ada_b.reshape(depth, 1, e))


def _discretise_kernel(are_ref, aim_ref, ldt_ref, bre_ref, bim_ref,
                       bbr_ref, bbi_ref, pwr_ref, pwi_ref, *, n_pow):
    a_re = are_ref[0]
    a_im = aim_ref[0]
    dt = jnp.exp(ldt_ref[0])
    mag = jnp.exp(a_re * dt)
    abar_re = mag * jnp.cos(a_im * dt)
    abar_im = mag * jnp.sin(a_im * dt)
    nr = abar_re - 1.0
    ni = abar_im
    den = a_re * a_re + a_im * a_im
    fr = (nr * a_re + ni * a_im) / den
    fi = (ni * a_re - nr * a_im) / den
    b_re = bre_ref[0]
    b_im = bim_ref[0]
    bbr_ref[0] = fr[:, None, :] * b_re - fi[:, None, :] * b_im
    bbi_ref[0] = fr[:, None, :] * b_im + fi[:, None, :] * b_re
    pr, pi = abar_re, abar_im
    pwr_ref[0, 0] = pr
    pwi_ref[0, 0] = pi
    for t in range(1, n_pow):
        pr, pi = abar_re * pr - abar_im * pi, abar_re * pi + abar_im * pr
        pwr_ref[0, t] = pr
        pwi_ref[0, t] = pi


def _discretise(a_re, a_im, log_dt, b_re, b_im, n_pow):
    depth, g, p = a_re.shape
    gc = b_re.shape[-1]
    b_re_t = jnp.swapaxes(b_re, 2, 3)
    b_im_t = jnp.swapaxes(b_im, 2, 3)
    spec_a = pl.BlockSpec((1, g, p), lambda l: (l, 0, 0))
    spec_b = pl.BlockSpec((1, g, gc, p), lambda l: (l, 0, 0, 0))
    spec_p = pl.BlockSpec((1, n_pow, g, p), lambda l: (l, 0, 0, 0))
    return pl.pallas_call(
        functools.partial(_discretise_kernel, n_pow=n_pow),
        grid=(depth,),
        in_specs=[spec_a, spec_a, pl.BlockSpec((1, g, 1), lambda l: (l, 0, 0)), spec_b, spec_b],
        out_specs=[spec_b, spec_b, spec_p, spec_p],
        out_shape=[jax.ShapeDtypeStruct((depth, g, gc, p), F32)] * 2
        + [jax.ShapeDtypeStruct((depth, n_pow, g, p), F32)] * 2,
        name="s5_discretise",
    )(a_re, a_im, log_dt.reshape(depth, g, 1), b_re_t, b_im_t)


def _block_diag_in(bbar, groups_per_block):
    depth, g, gc, p = bbar.shape
    nb = g // groups_per_block
    eye = jnp.eye(groups_per_block, dtype=bbar.dtype)
    b = bbar.reshape(depth, nb, groups_per_block, gc, p)
    out = b[:, :, :, :, None, :] * eye[None, None, :, None, :, None]
    return out.reshape(depth, nb, groups_per_block * gc, groups_per_block * p)


def _block_diag_out(c, groups_per_block):
    depth, g, gc, p = c.shape
    nb = g // groups_per_block
    eye = jnp.eye(groups_per_block, dtype=c.dtype)
    ct = jnp.swapaxes(c, 2, 3).reshape(depth, nb, groups_per_block, p, gc)
    out = ct[:, :, :, :, None, :] * eye[None, None, :, None, :, None]
    return out.reshape(depth, nb, groups_per_block * p, groups_per_block * gc)


MIXER_LAYER_WEIGHTS = ("g1", "win", "wgate", "bre", "bim", "cre", "cim", "pwr", "pwi", "dskip", "wglu",
                       "wao", "wout")
MIXER_SHARED = ("gk", "gv", "hmask")
MIXER_TABLES = ("cos", "sina", "sinb", "cq", "sq")
MIXER_STATE_IN = ("s0r", "s0i", "pk", "pv")
MIXER_OUT = ("xo", "ko", "vo", "sro", "sio")
MIXER_SCRATCH = ("h", "u", "up", "xr", "xi", "yp", "sinr", "sini", "carr", "cari", "kb", "vb", "kd", "vd", "ao")


def _mixer_kernel(*refs, nb, tl, nch, t_steps, n_tiles, layer, cached):
    r = _named(("x", "mod") + MIXER_LAYER_WEIGHTS + ("sink",) + MIXER_SHARED + MIXER_TABLES + MIXER_STATE_IN
               + MIXER_OUT + MIXER_SCRATCH, refs)
    tm = nb * tl
    d = r.x.shape[-1]
    ssm_w = r.up.shape[-1]
    n_state = r.xr.shape[-1]
    attn_w = N_HEADS * HEAD_DIM
    kv_w = N_KV_HEADS * HEAD_DIM
    first = pl.program_id(0) % n_tiles == 0

    x3 = r.x[...]
    mod = r.mod[...]
    h3 = _rms_modulate(x3, r.g1[...], mod[:, :, 0:d], mod[:, :, d:2 * d])
    hb = h3.reshape(tm, d).astype(BF16)
    r.h[...] = hb
    proj = _dot(hb, r.win[...])
    u = proj[:, 0:ssm_w]
    for c in range(ssm_w // LANES):
        r.u[c] = u[:, c * LANES:(c + 1) * LANES]
    q = proj[:, ssm_w:ssm_w + attn_w]
    k = proj[:, ssm_w + attn_w:ssm_w + attn_w + kv_w]
    v = proj[:, ssm_w + attn_w + kv_w:ssm_w + attn_w + 2 * kv_w]

    for t in range(t_steps):
        for c in range(ssm_w // LANES):
            r.up[t * nch:(t + 1) * nch, c * LANES:(c + 1) * LANES] = r.u[c, pl.ds(t, nch, stride=t_steps), :]
    n_in_blocks = r.bre.shape[0]
    kin = ssm_w // n_in_blocks
    nin = n_state // n_in_blocks
    for j in range(n_in_blocks):
        ub = r.up[:, j * kin:(j + 1) * kin].astype(BF16)
        r.xr[:, j * nin:(j + 1) * nin] = _dot(ub, r.bre[j])
        r.xi[:, j * nin:(j + 1) * nin] = _dot(ub, r.bim[j])

    def scan_block(lo, s_r, s_i, store):
        a_r = jnp.broadcast_to(r.pwr[0:1, lo:lo + SCAN_LANES], (nch, SCAN_LANES))
        a_i = jnp.broadcast_to(r.pwi[0:1, lo:lo + SCAN_LANES], (nch, SCAN_LANES))
        for t in range(t_steps):
            rows = slice(t * nch, (t + 1) * nch)
            in_r = r.xr[rows, lo:lo + SCAN_LANES]
            in_i = r.xi[rows, lo:lo + SCAN_LANES]
            s_r, s_i = a_r * s_r - a_i * s_i + in_r, a_r * s_i + a_i * s_r + in_i
            if store:
                r.xr[rows, lo:lo + SCAN_LANES] = s_r
                r.xi[rows, lo:lo + SCAN_LANES] = s_i
        return s_r, s_i

    if not cached:
        @pl.when(first)
        def _():
            r.carr[...] = r.s0r[0]
            r.cari[...] = r.s0i[0]

        zero = jnp.zeros((nch, SCAN_LANES), F32)
        for lo in range(0, n_state, SCAN_LANES):
            f_r, f_i = scan_block(lo, zero, zero, store=False)
            at_r = r.pwr[t_steps - 1:t_steps, lo:lo + SCAN_LANES]
            at_i = r.pwi[t_steps - 1:t_steps, lo:lo + SCAN_LANES]
            c_r = r.carr[:, lo:lo + SCAN_LANES]
            c_i = r.cari[:, lo:lo + SCAN_LANES]
            for ch in range(nch):
                r.sinr[ch:ch + 1, lo:lo + SCAN_LANES] = c_r
                r.sini[ch:ch + 1, lo:lo + SCAN_LANES] = c_i
                c_r, c_i = (at_r * c_r - at_i * c_i + f_r[ch:ch + 1],
                            at_r * c_i + at_i * c_r + f_i[ch:ch + 1])
            r.carr[:, lo:lo + SCAN_LANES] = c_r
            r.cari[:, lo:lo + SCAN_LANES] = c_i
        r.sro[0] = r.carr[...]
        r.sio[0] = r.cari[...]
    else:
        r.sinr[...] = r.s0r[0]
        r.sini[...] = r.s0i[0]

    for lo in range(0, n_state, SCAN_LANES):
        f_r, f_i = scan_block(lo, r.sinr[:, lo:lo + SCAN_LANES], r.sini[:, lo:lo + SCAN_LANES], store=True)
        if cached:
            r.sro[0, :, lo:lo + SCAN_LANES] = f_r
            r.sio[0, :, lo:lo + SCAN_LANES] = f_i

    n_out_blocks = r.cre.shape[0]
    kout = n_state // n_out_blocks
    nout = ssm_w // n_out_blocks
    for j in range(n_out_blocks):
        sr = r.xr[:, j * kout:(j + 1) * kout].astype(BF16)
        si = r.xi[:, j * kout:(j + 1) * kout].astype(BF16)
        yb = _dot(sr, r.cre[j]) - _dot(si, r.cim[j])
        for c in range(nout // LANES):
            r.yp[j * (nout // LANES) + c] = yb[:, c * LANES:(c + 1) * LANES]
    y = jnp.concatenate(
        [jnp.concatenate([r.yp[c, pl.ds(ch, t_steps, stride=nch), :] for ch in range(nch)], axis=0)
         for c in range(ssm_w // LANES)], axis=1)
    y = y + r.dskip[...] * u
    glu = _dot(_gelu_tanh(y).astype(BF16), r.wglu[...])
    branch_a = glu[:, 0:d] * _sigmoid(glu[:, d:2 * d])

    cos3 = r.cos[...]
    sina3 = r.sina[...]
    sinb3 = r.sinb[...]
    k_rolled_down = pltpu.roll(k, LANES - ROPE_DIM // 2, axis=1).reshape(nb, tl, LANES)
    k_rolled_up = pltpu.roll(k, ROPE_DIM // 2, axis=1).reshape(nb, tl, LANES)
    k3 = k.reshape(nb, tl, LANES) * cos3 + k_rolled_down * sina3 + k_rolled_up * sinb3
    v3 = v.reshape(nb, tl, LANES)
    kr = r.ko.shape[1]
    r.ko[...] = k3[:, tl - kr:, :]
    r.vo[...] = v3[:, tl - kr:, :]
    if not cached:
        @pl.when(first)
        def _():
            r.kb[0:WINDOW, :] = jnp.zeros((WINDOW, LANES), F32)
            r.vb[0:WINDOW, :] = jnp.zeros((WINDOW, LANES), F32)

        r.kb[WINDOW:WINDOW + tm, :] = k3.reshape(tm, LANES)
        r.vb[WINDOW:WINDOW + tm, :] = v
        kall = r.kb[...]
        vall = r.vb[...]
        band = WINDOW + CHUNK
        n_rows = CHUNK
        n_blocks = tm // CHUNK
    else:
        r.kb[:, 0:WINDOW, :] = r.pk[...]
        r.kb[:, WINDOW:WINDOW + tl, :] = k3
        r.vb[:, 0:WINDOW, :] = r.pv[...]
        r.vb[:, WINDOW:WINDOW + tl, :] = v3
        band = WINDOW + tl
        n_rows = tl
        n_blocks = nb
        kall = r.kb[...].reshape(nb * band, LANES)
        vall = r.vb[...].reshape(nb * band, LANES)
    group_w = 2 * LANES
    k_spread = _dot(kall.astype(BF16), r.gk[...]).astype(BF16)
    v_spread = _dot(vall.astype(BF16), r.gv[...]).astype(BF16)
    for j in range(N_KV_HEADS):
        r.kd[j] = k_spread[:, j * group_w:(j + 1) * group_w]
        r.vd[j, :, 0:LANES] = v_spread[:, j * LANES:(j + 1) * LANES]
        r.vd[j, :, LANES:group_w] = jnp.ones((kall.shape[0], LANES), BF16)
    if not cached:
        r.kb[0:WINDOW, :] = r.kb[tm:tm + WINDOW, :]
        r.vb[0:WINDOW, :] = r.vb[tm:tm + WINDOW, :]

    scale = HEAD_DIM ** -0.5
    cq3 = r.cq[...]
    sq3 = r.sq[...]
    q_groups = []
    for j in range(N_KV_HEADS):
        c0 = q[:, j * group_w:j * group_w + LANES].reshape(nb, tl, LANES)
        c1 = q[:, j * group_w + LANES:(j + 1) * group_w].reshape(nb, tl, LANES)
        r0c = (c0 * cq3 - c1 * sq3).reshape(tm, LANES)
        r1c = (c1 * cq3 + c0 * sq3).reshape(tm, LANES)
        q_groups.append(jnp.concatenate([r0c, r1c], axis=1) * scale)
    q_per_kv = N_HEADS // N_KV_HEADS
    lane = lax.broadcasted_iota(jnp.int32, (1, LANES), 1)
    low_half = lane < HEAD_DIM
    key_idx = lax.broadcasted_iota(jnp.int32, (1, band), 1)
    for blk in range(n_blocks):
        r0 = blk * n_rows
        b0 = blk * (CHUNK if not cached else band)
        for j in range(N_KV_HEADS):
            qg = q_groups[j][r0:r0 + n_rows]
            lhs = jnp.concatenate([qg * r.hmask[g:g + 1, :] for g in range(q_per_kv)],
                                  axis=0).astype(BF16)
            s = _dot_nt(lhs, r.kd[j, b0:b0 + band, :])
            if not cached and blk < WINDOW // CHUNK:
                first_valid = jnp.where(first, (WINDOW // CHUNK - blk) * CHUNK, 0)
                s = jnp.where(key_idx >= first_valid, s, NEG_INF)
            sink = jnp.concatenate(
                [jnp.full((n_rows, 1), r.sink[layer, j * q_per_kv + g], F32) for g in range(q_per_kv)], axis=0)
            m = jnp.maximum(jnp.max(s, axis=-1, keepdims=True), sink)
            pexp = jnp.exp(s - m)
            o2 = _dot(pexp.astype(BF16), r.vd[j, b0:b0 + band, :])
            o = o2[:, 0:LANES] / (o2[:, LANES:group_w] + jnp.exp(sink - m))
            for p in range(q_per_kv // 2):
                lo_rows = o[(2 * p) * n_rows:(2 * p + 1) * n_rows]
                hi_rows = o[(2 * p + 1) * n_rows:(2 * p + 2) * n_rows]
                col = (j * (q_per_kv // 2) + p) * LANES
                r.ao[r0:r0 + n_rows, col:col + LANES] = jnp.where(low_half, lo_rows, hi_rows).astype(BF16)
    branch_b = _dot(r.ao[...], r.wao[...])

    gates = _dot(r.h[...], r.wgate[...])
    merged = _sigmoid(gates[:, 0:d]) * branch_a + _sigmoid(gates[:, d:2 * d]) * branch_b
    out = _dot(merged.astype(BF16), r.wout[...])
    r.xo[...] = x3 + mod[:, :, 2 * d:3 * d] * out.reshape(nb, tl, d)


def _mixer(x, mod, w, sink, shared, tables, s0_re, s0_im, past_k, past_v, *, layer, nb, tl, cached):
    b, l, d = x.shape
    n_tiles = l // tl
    n_steps = (b // nb) * n_tiles
    tm = nb * tl
    ssm_w = w["dskip"].shape[-1]
    n_state = w["pwr"].shape[-1]
    if cached:
        nch, t_steps = nb, tl
        kr = tl
        band_rows = (nb, WINDOW + tl, LANES)
        dup_rows = nb * (WINDOW + tl)
    else:
        nch, t_steps = SUBLANES, tl // SUBLANES
        kr = WINDOW
        band_rows = (WINDOW + tm, LANES)
        dup_rows = WINDOW + tm
    n_s0 = s0_re.shape[2]

    def tile_map(i):
        return (i // n_tiles, i % n_tiles, 0)

    def layer_batch_map(i):
        return (layer, i // n_tiles, 0, 0)

    def batch_map(i):
        return (i // n_tiles, 0, 0)

    pos_spec = pl.BlockSpec((1, tl, LANES), lambda i: (0, i % n_tiles, 0))
    state_in_spec = pl.BlockSpec((None, 1, n_s0, n_state), layer_batch_map)
    state_out_spec = pl.BlockSpec((1, n_s0, n_state), batch_map)
    in_specs = (
        [pl.BlockSpec((nb, tl, d), tile_map),
         pl.BlockSpec((None, nb, 1, N_ADA * d), layer_batch_map)]
        + [_layer_block(w[n], layer) for n in MIXER_LAYER_WEIGHTS]
        + [pl.BlockSpec(memory_space=pltpu.SMEM)]
        + [_whole(shared[n]) for n in MIXER_SHARED]
        + [pos_spec] * len(MIXER_TABLES)
        + [state_in_spec, state_in_spec,
           pl.BlockSpec((None, nb) + past_k.shape[2:], layer_batch_map),
           pl.BlockSpec((None, nb) + past_v.shape[2:], layer_batch_map)]
    )
    out_specs = [
        pl.BlockSpec((nb, tl, d), tile_map),
        pl.BlockSpec((nb, kr, LANES), batch_map),
        pl.BlockSpec((nb, kr, LANES), batch_map),
        state_out_spec, state_out_spec,
    ]
    out_shape = [
        jax.ShapeDtypeStruct((b, l, d), F32),
        jax.ShapeDtypeStruct((b, kr, LANES), F32),
        jax.ShapeDtypeStruct((b, kr, LANES), F32),
        jax.ShapeDtypeStruct(s0_re.shape[1:], F32),
        jax.ShapeDtypeStruct(s0_im.shape[1:], F32),
    ]
    scratch = [
        pltpu.VMEM((tm, d), BF16),
        pltpu.VMEM((ssm_w // LANES, tm, LANES), F32),
        pltpu.VMEM((tm, ssm_w), F32),
        pltpu.VMEM((tm, n_state), F32),
        pltpu.VMEM((tm, n_state), F32),
        pltpu.VMEM((ssm_w // LANES, tm, LANES), F32),
        pltpu.VMEM((nch, n_state), F32),
        pltpu.VMEM((nch, n_state), F32),
        pltpu.VMEM((1, n_state), F32),
        pltpu.VMEM((1, n_state), F32),
        pltpu.VMEM(band_rows, F32),
        pltpu.VMEM(band_rows, F32),
        pltpu.VMEM((N_KV_HEADS, dup_rows, 2 * LANES), BF16),
        pltpu.VMEM((N_KV_HEADS, dup_rows, 2 * LANES), BF16),
        pltpu.VMEM((tm, N_HEADS * HEAD_DIM), BF16),
    ]
    kern = functools.partial(_mixer_kernel, nb=nb, tl=tl, nch=nch, t_steps=t_steps,
                             n_tiles=n_tiles, layer=layer, cached=cached)
    return pl.pallas_call(
        kern,
        grid=(n_steps,),
        in_specs=in_specs,
        out_specs=out_specs,
        out_shape=out_shape,
        scratch_shapes=scratch,
        compiler_params=pltpu.CompilerParams(dimension_semantics=("arbitrary",),
                                             vmem_limit_bytes=VMEM_LIMIT_BYTES),
        name="mixer_cached" if cached else "mixer_banded",
    )(x, mod, *[w[n] for n in MIXER_LAYER_WEIGHTS], sink, *[shared[n] for n in MIXER_SHARED], *tables,
      s0_re, s0_im, past_k, past_v)


FFN_LAYER_WEIGHTS = ("g2", "wup", "cw", "cb", "wdn")


def _ffn_kernel(*refs, nb, tl, n_tiles, col_w, final):
    r = _named(("x", "mod") + FFN_LAYER_WEIGHTS + ("fg", "prev", "xo", "cs", "ub", "carry"), refs)
    tm = nb * tl
    d = r.x.shape[-1]
    d_ff = r.wdn.shape[0]
    pad = SUBLANES
    first = pl.program_id(0) % n_tiles == 0

    @pl.when(first)
    def _():
        r.carry[...] = r.prev[...]

    x3 = r.x[...]
    mod = r.mod[...]
    h3 = _rms_modulate(x3, r.g2[...], mod[:, :, 3 * d:4 * d], mod[:, :, 4 * d:5 * d])
    hb = h3.reshape(tm, d).astype(BF16)

    def conv_cols(c0):
        up = _dot(hb, r.wup[:, c0:c0 + col_w]).reshape(nb, tl, col_w)
        r.ub[:, pad - 2:pad, :] = r.carry[:, :, c0:c0 + col_w]
        r.ub[:, pad:pad + tl, :] = up
        r.carry[:, :, c0:c0 + col_w] = r.ub[:, pad + tl - 2:pad + tl, :]
        w = r.cw[:, c0:c0 + col_w]
        out = r.cb[:, c0:c0 + col_w][None] + r.ub[:, pad - 2:pad - 2 + tl, :] * w[0:1][None]
        out = out + r.ub[:, pad - 1:pad - 1 + tl, :] * w[1:2][None]
        out = out + up * w[2:3][None]
        return out.reshape(tm, col_w)

    acc = jnp.zeros((tm, d), F32)
    for c0 in range(0, d_ff, col_w):
        val = conv_cols(c0)
        gate = conv_cols(d_ff + c0)
        act = (gate * _sigmoid(gate) * val).astype(BF16)
        acc = acc + _dot(act, r.wdn[c0:c0 + col_w, :])
    r.cs[...] = r.carry[...]
    xn = x3 + mod[:, :, 5 * d:6 * d] * acc.reshape(nb, tl, d)
    if final:
        xn = xn * lax.rsqrt(jnp.mean(xn * xn, axis=-1, keepdims=True) + RMS_EPS) * r.fg[...][None]
    r.xo[...] = xn


def _ffn_col_width(d_ff):
    n = d_ff // LANES
    for parts in range(2, n + 1):
        if n % parts == 0:
            return (n // parts) * LANES
    return d_ff


def _ffn(x, mod, w, final_g, conv_prev, *, layer, nb, tl, final):
    b, l, d = x.shape
    n_tiles = l // tl
    n_steps = (b // nb) * n_tiles
    c = w["wup"].shape[-1]
    col_w = _ffn_col_width(w["wdn"].shape[1])

    def tile_map(i):
        return (i // n_tiles, i % n_tiles, 0)

    def layer_batch_map(i):
        return (layer, i // n_tiles, 0, 0)

    kern = functools.partial(_ffn_kernel, nb=nb, tl=tl, n_tiles=n_tiles, col_w=col_w, final=final)
    return pl.pallas_call(
        kern,
        grid=(n_steps,),
        in_specs=[pl.BlockSpec((nb, tl, d), tile_map),
                  pl.BlockSpec((None, nb, 1, N_ADA * d), layer_batch_map)]
        + [_layer_block(w[n], layer) for n in FFN_LAYER_WEIGHTS]
        + [_whole(final_g),
           pl.BlockSpec((None, nb, CONV_WIDTH - 1, c), layer_batch_map)],
        out_specs=[
            pl.BlockSpec((nb, tl, d), tile_map),
            pl.BlockSpec((nb, CONV_WIDTH - 1, c), lambda i: (i // n_tiles, 0, 0)),
        ],
        out_shape=[
            jax.ShapeDtypeStruct((b, l, d), F32),
            jax.ShapeDtypeStruct((b, CONV_WIDTH - 1, c), F32),
        ],
        scratch_shapes=[
            pltpu.VMEM((nb, SUBLANES + tl, col_w), F32),
            pltpu.VMEM((nb, CONV_WIDTH - 1, c), F32),
        ],
        compiler_params=pltpu.CompilerParams(dimension_semantics=("arbitrary",),
                                             vmem_limit_bytes=VMEM_LIMIT_BYTES),
        name="conv_ffn_final" if final else "conv_ffn",
    )(x, mod, *[w[n] for n in FFN_LAYER_WEIGHTS], final_g, conv_prev)


def _rotary_tables(pos):
    half = ROPE_DIM // 2
    inv_freq = ROPE_THETA ** (-(jnp.arange(half, dtype=F32) * 2.0) / ROPE_DIM)
    ang = pos.astype(F32)[:, None] * inv_freq[None, :]
    cos = jnp.cos(ang)
    sin = jnp.sin(ang)
    n = pos.shape[0]
    rest = HEAD_DIM - ROPE_DIM
    cos_h = jnp.concatenate([cos, cos, jnp.ones((n, rest), F32)], axis=1)
    sina_h = jnp.concatenate([-sin, jnp.zeros((n, half + rest), F32)], axis=1)
    sinb_h = jnp.concatenate([jnp.zeros((n, half), F32), sin, jnp.zeros((n, rest), F32)], axis=1)
    reps = LANES // HEAD_DIM
    key_tabs = tuple(jnp.tile(t, (1, reps))[None] for t in (cos_h, sina_h, sinb_h))
    q_per_kv = N_HEADS // N_KV_HEADS
    n_rot = q_per_kv * half
    cos_q = jnp.concatenate([jnp.tile(cos, (1, q_per_kv)), jnp.ones((n, LANES - n_rot), F32)], axis=1)
    sin_q = jnp.concatenate([jnp.tile(sin, (1, q_per_kv)), jnp.zeros((n, LANES - n_rot), F32)], axis=1)
    return key_tabs + (cos_q[None], sin_q[None])


def _group_layout():
    q_per_kv = N_HEADS // N_KV_HEADS
    half = ROPE_DIM // 2
    rest = HEAD_DIM - ROPE_DIM
    n_rot = q_per_kv * half
    slots, dims = [], []
    for col in range(2):
        for lane in range(LANES):
            if lane < n_rot:
                slots.append(lane // half)
                dims.append(lane % half + half * col)
            else:
                r = lane - n_rot
                slots.append(col * (q_per_kv // 2) + r // rest)
                dims.append(ROPE_DIM + r % rest)
    return slots, dims


def _attention_constants():
    q_per_kv = N_HEADS // N_KV_HEADS
    kv_w = N_KV_HEADS * HEAD_DIM
    slots, dims = _group_layout()
    group_w = len(slots)
    q_perm = [(j * q_per_kv + slots[n]) * HEAD_DIM + dims[n] for j in range(N_KV_HEADS) for n in range(group_w)]
    dims_a = jnp.asarray(dims)
    slots_a = jnp.asarray(slots)
    rows = jnp.arange(kv_w)[:, None]
    gather_k = jnp.concatenate(
        [(rows == j * HEAD_DIM + dims_a[None, :]) for j in range(N_KV_HEADS)], axis=1).astype(BF16)
    pair_dim = jnp.arange(LANES) % HEAD_DIM
    gather_v = jnp.concatenate(
        [(rows == j * HEAD_DIM + pair_dim[None, :]) for j in range(N_KV_HEADS)], axis=1).astype(BF16)
    head_mask = (slots_a[None, :] == jnp.arange(q_per_kv)[:, None]).astype(F32)
    return jnp.asarray(q_perm), dict(gk=gather_k, gv=gather_v, hmask=head_mask)


def kernel(x_prompt, x_sample, cache_k, cache_v, state_ssm_re, state_ssm_im, state_conv, c_prompt, c_sample, ada_w, ada_b, norm1_g, norm2_g, w_in, ssm_a_re, ssm_a_im, ssm_log_dt, ssm_b_re, ssm_b_im, ssm_c_re, ssm_c_im, ssm_d, w_glu, attn_sink, w_attn_o, w_out, ffn_w_up, ffn_conv_w, ffn_conv_b, ffn_w_down, final_g):
    bp, lp, d = x_prompt.shape
    bs, ls, _ = x_sample.shape
    depth = w_in.shape[0]
    groups, n_p = ssm_a_re.shape[1], ssm_a_re.shape[2]
    n_state = groups * n_p
    ssm_w = ssm_d.shape[-1]
    attn_w = N_HEADS * HEAD_DIM
    kv_w = N_KV_HEADS * HEAD_DIM
    c_ff = ffn_w_up.shape[-1]
    n_qkv = ssm_w + attn_w + 2 * kv_w
    tl_p = min(PROMPT_TILE, lp)
    n_pow = max(tl_p // SUBLANES, ls)

    mod = _ada_modulation(jnp.concatenate([c_prompt, c_sample], axis=0), ada_w, ada_b)
    mod_p = mod[:, :bp].reshape(depth, bp, 1, N_ADA * d)
    mod_s = mod[:, bp:].reshape(depth, bs, 1, N_ADA * d)
    bbar_re, bbar_im, pw_re, pw_im = _discretise(ssm_a_re, ssm_a_im, ssm_log_dt, ssm_b_re, ssm_b_im, n_pow)
    gpb = MXU_DIM // SSM_GROUP
    q_perm, attn_shared = _attention_constants()

    w_mix = dict(
        g1=norm1_g[:, None, :],
        win=jnp.concatenate([w_in[:, :, :ssm_w], w_in[:, :, ssm_w + q_perm],
                             w_in[:, :, ssm_w + attn_w:n_qkv]], axis=2).astype(BF16),
        wgate=w_in[:, :, n_qkv:].astype(BF16),
        bre=_block_diag_in(bbar_re, gpb).astype(BF16),
        bim=_block_diag_in(bbar_im, gpb).astype(BF16),
        cre=_block_diag_out(ssm_c_re, gpb).astype(BF16),
        cim=_block_diag_out(ssm_c_im, gpb).astype(BF16),
        pwr=pw_re.reshape(depth, n_pow, n_state),
        pwi=pw_im.reshape(depth, n_pow, n_state),
        dskip=ssm_d[:, None, :],
        wglu=w_glu.astype(BF16),
        wao=w_attn_o.astype(BF16),
        wout=w_out.astype(BF16),
    )
    w_ffn = dict(
        g2=norm2_g[:, None, :],
        wup=ffn_w_up.astype(BF16),
        cw=ffn_conv_w,
        cb=ffn_conv_b[:, None, :],
        wdn=ffn_w_down.astype(BF16),
    )
    fg = final_g[None]

    tabs_p = _rotary_tables(jnp.arange(lp))
    tabs_s = _rotary_tables(PAST_LEN + jnp.arange(ls))
    zero_state = jnp.zeros((depth, bp, 1, n_state), F32)
    zero_conv = jnp.zeros((depth, bp, CONV_WIDTH - 1, c_ff), F32)
    zero_past = jnp.zeros((depth, bp, SUBLANES, LANES), F32)
    s0_re = state_ssm_re.reshape(depth, 1, bs, n_state)
    s0_im = state_ssm_im.reshape(depth, 1, bs, n_state)
    past_k = cache_k.reshape(depth, bs, WINDOW, kv_w)
    past_v = cache_v.reshape(depth, bs, WINDOW, kv_w)

    xp, xs = x_prompt, x_sample
    outs_p = [[] for _ in range(5)]
    outs_s = [[] for _ in range(5)]
    for i in range(depth):
        last = i == depth - 1
        xp, nk, nv, sr, si = _mixer(xp, mod_p, w_mix, attn_sink, attn_shared, tabs_p, zero_state, zero_state,
                                    zero_past, zero_past, layer=i, nb=1, tl=tl_p, cached=False)
        xp, cs = _ffn(xp, mod_p, w_ffn, fg, zero_conv, layer=i, nb=1, tl=tl_p, final=last)
        for lst, val in zip(outs_p, (
                nk.reshape(bp, WINDOW, N_KV_HEADS, HEAD_DIM), nv.reshape(bp, WINDOW, N_KV_HEADS, HEAD_DIM),
                sr.reshape(bp, groups, n_p), si.reshape(bp, groups, n_p), cs)):
            lst.append(val)

        xs, nk, nv, sr, si = _mixer(xs, mod_s, w_mix, attn_sink, attn_shared, tabs_s, s0_re, s0_im,
                                    past_k, past_v, layer=i, nb=bs, tl=ls, cached=True)
        xs, cs = _ffn(xs, mod_s, w_ffn, fg, state_conv, layer=i, nb=bs, tl=ls, final=last)
        for lst, val in zip(outs_s, (
                nk.reshape(bs, ls, N_KV_HEADS, HEAD_DIM), nv.reshape(bs, ls, N_KV_HEADS, HEAD_DIM),
                sr.reshape(bs, groups, n_p), si.reshape(bs, groups, n_p), cs)):
            lst.append(val)

    return (xp, xs) + tuple(jnp.stack(o, axis=0) for o in outs_p) + tuple(jnp.stack(o, axis=0) for o in outs_s)
```

```python
import functools
import math
import types

import jax
import jax.numpy as jnp
from jax import lax
from jax.experimental import pallas as pl
from jax.experimental.pallas import tpu as pltpu

CHUNK = 64
SSM_GROUP = 16
SSM_STATE = 64
N_HEADS = 8
N_KV_HEADS = 2
HEAD_DIM = 64
WINDOW = 128
ROPE_DIM = HEAD_DIM // 4
ROPE_THETA = 500000.0
CONV_WIDTH = 3
N_ADA = 6
RMS_EPS = 1e-6
NEG_INF = -1e30
PAST_LEN = 2048

LANES = 128
SUBLANES = 8
MXU_DIM = 256
VMEM_LIMIT_BYTES = 58 * 1024 * 1024

PROMPT_TILE = 512
SCAN_LANES = 512
FFN_COLUMN_BLOCKS = 1
BF16 = jnp.bfloat16
F32 = jnp.float32


def _dot(a, b):
    return jnp.dot(a, b, preferred_element_type=F32)


def _dot_nt(a, b):
    return lax.dot_general(a, b, (((1,), (1,)), ((), ())), preferred_element_type=F32)


def _sigmoid(x):
    return 0.5 * jnp.tanh(0.5 * x) + 0.5


def _gelu_tanh(x):
    c = math.sqrt(2.0 / math.pi)
    return 0.5 * x * (1.0 + jnp.tanh(c * (x + 0.044715 * (x * x * x))))


def _rms_modulate(x3, g_row, shift3, scale3):
    y = x3 * lax.rsqrt(jnp.mean(x3 * x3, axis=-1, keepdims=True) + RMS_EPS)
    y = y * g_row[None]
    return y * (1.0 + scale3) + shift3


def _named(names, refs):
    assert len(names) == len(refs), (len(names), len(refs))
    return types.SimpleNamespace(**dict(zip(names, refs)))


def _layer_block(a, layer):
    zeros = (0,) * (a.ndim - 1)
    return pl.BlockSpec((None,) + a.shape[1:], lambda i: (layer,) + zeros, pipeline_mode=pl.Buffered(1))


def _whole(a):
    zeros = (0,) * a.ndim
    return pl.BlockSpec(a.shape, lambda i: zeros, pipeline_mode=pl.Buffered(1))


def _ada_kernel(c_ref, w_ref, b_ref, o_ref):
    c = c_ref[...]
    a = (c * _sigmoid(c)).astype(BF16)
    o_ref[0] = _dot(a, w_ref[0].astype(BF16)) + b_ref[0]


def _ada_modulation(c_all, ada_w, ada_b):
    depth, d, e = ada_w.shape
    nb = c_all.shape[0]
    col = d
    return pl.pallas_call(
        _ada_kernel,
        grid=(depth, e // col),
        in_specs=[
            pl.BlockSpec((nb, d), lambda l, j: (0, 0)),
            pl.BlockSpec((1, d, col), lambda l, j: (l, 0, j)),
            pl.BlockSpec((1, 1, col), lambda l, j: (l, 0, j)),
        ],
        out_specs=pl.BlockSpec((1, nb, col), lambda l, j: (l, 0, j)),
        out_shape=jax.ShapeDtypeStruct((depth, nb, e), F32),
        name="ada_modulation",
    )(c_all, ada_w, ada_b.reshape(depth, 1, e))


def _discretise_kernel(are_ref, aim_ref, ldt_ref, bre_ref, bim_ref,
                       bbr_ref, bbi_ref, pwr_ref, pwi_ref, *, n_pow):
    a_re = are_ref[0]
    a_im = aim_ref[0]
    dt = jnp.exp(ldt_ref[0])
    mag = jnp.exp(a_re * dt)
    abar_re = mag * jnp.cos(a_im * dt)
    abar_im = mag * jnp.sin(a_im * dt)
    nr = abar_re - 1.0
    ni = abar_im
    den = a_re * a_re + a_im * a_im
    fr = (nr * a_re + ni * a_im) / den
    fi = (ni * a_re - nr * a_im) / den
    b_re = bre_ref[0]
    b_im = bim_ref[0]
    bbr_ref[0] = fr[:, None, :] * b_re - fi[:, None, :] * b_im
    bbi_ref[0] = fr[:, None, :] * b_im + fi[:, None, :] * b_re
    pr, pi = abar_re, abar_im
    pwr_ref[0, 0] = pr
    pwi_ref[0, 0] = pi
    for t in range(1, n_pow):
        pr, pi = abar_re * pr - abar_im * pi, abar_re * pi + abar_im * pr
        pwr_ref[0, t] = pr
        pwi_ref[0, t] = pi


def _discretise(a_re, a_im, log_dt, b_re, b_im, n_pow):
    depth, g, p = a_re.shape
    gc = b_re.shape[-1]
    b_re_t = jnp.swapaxes(b_re, 2, 3)
    b_im_t = jnp.swapaxes(b_im, 2, 3)
    spec_a = pl.BlockSpec((1, g, p), lambda l: (l, 0, 0))
    spec_b = pl.BlockSpec((1, g, gc, p), lambda l: (l, 0, 0, 0))
    spec_p = pl.BlockSpec((1, n_pow, g, p), lambda l: (l, 0, 0, 0))
    return pl.pallas_call(
        functools.partial(_discretise_kernel, n_pow=n_pow),
        grid=(depth,),
        in_specs=[spec_a, spec_a, pl.BlockSpec((1, g, 1), lambda l: (l, 0, 0)), spec_b, spec_b],
        out_specs=[spec_b, spec_b, spec_p, spec_p],
        out_shape=[jax.ShapeDtypeStruct((depth, g, gc, p), F32)] * 2
        + [jax.ShapeDtypeStruct((depth, n_pow, g, p), F32)] * 2,
        name="s5_discretise",
    )(a_re, a_im, log_dt.reshape(depth, g, 1), b_re_t, b_im_t)


def _block_diag(m, groups_per_block):
    depth, g, rows, cols = m.shape
    nb = g // groups_per_block
    m = m.reshape(depth, nb, groups_per_block, rows, cols)
    padded = [jnp.pad(m[:, :, j], ((0, 0), (0, 0), (0, 0), (j * cols, (groups_per_block - 1 - j) * cols)))
              for j in range(groups_per_block)]
    return jnp.stack(padded, axis=2).reshape(depth, nb, groups_per_block * rows, groups_per_block * cols)


def _block_diag_in(bbar, groups_per_block):
    return _block_diag(bbar.astype(BF16), groups_per_block)


def _block_diag_out(c, groups_per_block):
    return _block_diag(jnp.swapaxes(c, 2, 3).astype(BF16), groups_per_block)


MIXER_LAYER_WEIGHTS = ("g1", "win", "wgate", "bre", "bim", "cre", "cim", "pwr", "pwi", "dskip", "wglu",
                       "wao", "wout")
MIXER_SHARED = ("gk", "gv", "hmask")
MIXER_TABLES = ("cos", "sina", "sinb", "cq", "sq")
MIXER_STATE_IN = ("s0r", "s0i", "pk", "pv")
MIXER_OUT = ("xo", "ko", "vo", "sro", "sio")
MIXER_SCRATCH = ("h", "u", "up", "xr", "xi", "yp", "sinr", "sini", "carr", "cari", "kb", "vb", "kd", "vd", "ao")


def _mixer_kernel(*refs, nb, tl, nch, t_steps, n_tiles, layer, cached):
    r = _named(("x", "mod") + MIXER_LAYER_WEIGHTS + ("sink",) + MIXER_SHARED + MIXER_TABLES + MIXER_STATE_IN
               + MIXER_OUT + MIXER_SCRATCH, refs)
    tm = nb * tl
    d = r.x.shape[-1]
    ssm_w = r.up.shape[-1]
    n_state = r.xr.shape[-1]
    attn_w = N_HEADS * HEAD_DIM
    kv_w = N_KV_HEADS * HEAD_DIM
    first = pl.program_id(0) % n_tiles == 0

    x3 = r.x[...]
    mod = r.mod[...]
    h3 = _rms_modulate(x3, r.g1[...], mod[:, :, 0:d], mod[:, :, d:2 * d])
    hb = h3.reshape(tm, d).astype(BF16)
    r.h[...] = hb
    proj = _dot(hb, r.win[...])
    u = proj[:, 0:ssm_w]
    for c in range(ssm_w // LANES):
        r.u[c] = u[:, c * LANES:(c + 1) * LANES]
    q = proj[:, ssm_w:ssm_w + attn_w]
    k = proj[:, ssm_w + attn_w:ssm_w + attn_w + kv_w]
    v = proj[:, ssm_w + attn_w + kv_w:ssm_w + attn_w + 2 * kv_w]

    for t in range(t_steps):
        for c in range(ssm_w // LANES):
            r.up[t * nch:(t + 1) * nch, c * LANES:(c + 1) * LANES] = r.u[c, pl.ds(t, nch, stride=t_steps), :]
    n_in_blocks = r.bre.shape[0]
    kin = ssm_w // n_in_blocks
    nin = n_state // n_in_blocks
    for j in range(n_in_blocks):
        ub = r.up[:, j * kin:(j + 1) * kin].astype(BF16)
        r.xr[:, j * nin:(j + 1) * nin] = _dot(ub, r.bre[j])
        r.xi[:, j * nin:(j + 1) * nin] = _dot(ub, r.bim[j])

    def scan_block(lo, s_r, s_i, store):
        a_r = jnp.broadcast_to(r.pwr[0:1, lo:lo + SCAN_LANES], (nch, SCAN_LANES))
        a_i = jnp.broadcast_to(r.pwi[0:1, lo:lo + SCAN_LANES], (nch, SCAN_LANES))
        for t in range(t_steps):
            rows = slice(t * nch, (t + 1) * nch)
            in_r = r.xr[rows, lo:lo + SCAN_LANES]
            in_i = r.xi[rows, lo:lo + SCAN_LANES]
            s_r, s_i = a_r * s_r - a_i * s_i + in_r, a_r * s_i + a_i * s_r + in_i
            if store:
                r.xr[rows, lo:lo + SCAN_LANES] = s_r
                r.xi[rows, lo:lo + SCAN_LANES] = s_i
        return s_r, s_i

    if not cached:
        @pl.when(first)
        def _():
            r.carr[...] = r.s0r[0]
            r.cari[...] = r.s0i[0]

        zero = jnp.zeros((nch, SCAN_LANES), F32)
        for lo in range(0, n_state, SCAN_LANES):
            f_r, f_i = scan_block(lo, zero, zero, store=False)
            at_r = r.pwr[t_steps - 1:t_steps, lo:lo + SCAN_LANES]
            at_i = r.pwi[t_steps - 1:t_steps, lo:lo + SCAN_LANES]
            c_r = r.carr[:, lo:lo + SCAN_LANES]
            c_i = r.cari[:, lo:lo + SCAN_LANES]
            for ch in range(nch):
                r.sinr[ch:ch + 1, lo:lo + SCAN_LANES] = c_r
                r.sini[ch:ch + 1, lo:lo + SCAN_LANES] = c_i
                c_r, c_i = (at_r * c_r - at_i * c_i + f_r[ch:ch + 1],
                            at_r * c_i + at_i * c_r + f_i[ch:ch + 1])
            r.carr[:, lo:lo + SCAN_LANES] = c_r
            r.cari[:, lo:lo + SCAN_LANES] = c_i
        r.sro[0] = r.carr[...]
        r.sio[0] = r.cari[...]
    else:
        r.sinr[...] = r.s0r[0]
        r.sini[...] = r.s0i[0]

    for lo in range(0, n_state, SCAN_LANES):
        f_r, f_i = scan_block(lo, r.sinr[:, lo:lo + SCAN_LANES], r.sini[:, lo:lo + SCAN_LANES], store=True)
        if cached:
            r.sro[0, :, lo:lo + SCAN_LANES] = f_r
            r.sio[0, :, lo:lo + SCAN_LANES] = f_i

    n_out_blocks = r.cre.shape[0]
    kout = n_state // n_out_blocks
    nout = ssm_w // n_out_blocks
    for j in range(n_out_blocks):
        sr = r.xr[:, j * kout:(j + 1) * kout].astype(BF16)
        si = r.xi[:, j * kout:(j + 1) * kout].astype(BF16)
        yb = _dot(sr, r.cre[j]) - _dot(si, r.cim[j])
        for c in range(nout // LANES):
            r.yp[j * (nout // LANES) + c] = yb[:, c * LANES:(c + 1) * LANES]
    y = jnp.concatenate(
        [jnp.concatenate([r.yp[c, pl.ds(ch, t_steps, stride=nch), :] for ch in range(nch)], axis=0)
         for c in range(ssm_w // LANES)], axis=1)
    y = y + r.dskip[...] * u
    glu = _dot(_gelu_tanh(y).astype(BF16), r.wglu[...])
    branch_a = glu[:, 0:d] * _sigmoid(glu[:, d:2 * d])

    cos3 = r.cos[...]
    sina3 = r.sina[...]
    sinb3 = r.sinb[...]
    k_rolled_down = pltpu.roll(k, LANES - ROPE_DIM // 2, axis=1).reshape(nb, tl, LANES)
    k_rolled_up = pltpu.roll(k, ROPE_DIM // 2, axis=1).reshape(nb, tl, LANES)
    k3 = k.reshape(nb, tl, LANES) * cos3 + k_rolled_down * sina3 + k_rolled_up * sinb3
    v3 = v.reshape(nb, tl, LANES)
    kr = r.ko.shape[1]
    r.ko[...] = k3[:, tl - kr:, :]
    r.vo[...] = v3[:, tl - kr:, :]
    if not cached:
        @pl.when(first)
        def _():
            r.kb[0:WINDOW, :] = jnp.zeros((WINDOW, LANES), F32)
            r.vb[0:WINDOW, :] = jnp.zeros((WINDOW, LANES), F32)

        r.kb[WINDOW:WINDOW + tm, :] = k3.reshape(tm, LANES)
        r.vb[WINDOW:WINDOW + tm, :] = v
        kall = r.kb[...]
        vall = r.vb[...]
        band = WINDOW + CHUNK
        n_rows = CHUNK
        n_blocks = tm // CHUNK
    else:
        r.kb[:, 0:WINDOW, :] = r.pk[...]
        r.kb[:, WINDOW:WINDOW + tl, :] = k3
        r.vb[:, 0:WINDOW, :] = r.pv[...]
        r.vb[:, WINDOW:WINDOW + tl, :] = v3
        band = WINDOW + tl
        n_rows = tl
        n_blocks = nb
        kall = r.kb[...].reshape(nb * band, LANES)
        vall = r.vb[...].reshape(nb * band, LANES)
    group_w = 2 * LANES
    k_spread = _dot(kall.astype(BF16), r.gk[...]).astype(BF16)
    v_spread = _dot(vall.astype(BF16), r.gv[...]).astype(BF16)
    for j in range(N_KV_HEADS):
        r.kd[j] = k_spread[:, j * group_w:(j + 1) * group_w]
        r.vd[j, :, 0:LANES] = v_spread[:, j * LANES:(j + 1) * LANES]
        r.vd[j, :, LANES:group_w] = jnp.ones((kall.shape[0], LANES), BF16)
    if not cached:
        r.kb[0:WINDOW, :] = r.kb[tm:tm + WINDOW, :]
        r.vb[0:WINDOW, :] = r.vb[tm:tm + WINDOW, :]

    scale = HEAD_DIM ** -0.5
    cq3 = r.cq[...]
    sq3 = r.sq[...]
    q_groups = []
    for j in range(N_KV_HEADS):
        c0 = q[:, j * group_w:j * group_w + LANES].reshape(nb, tl, LANES)
        c1 = q[:, j * group_w + LANES:(j + 1) * group_w].reshape(nb, tl, LANES)
        r0c = (c0 * cq3 - c1 * sq3).reshape(tm, LANES)
        r1c = (c1 * cq3 + c0 * sq3).reshape(tm, LANES)
        q_groups.append(jnp.concatenate([r0c, r1c], axis=1) * scale)
    q_per_kv = N_HEADS // N_KV_HEADS
    lane = lax.broadcasted_iota(jnp.int32, (1, LANES), 1)
    low_half = lane < HEAD_DIM
    key_idx = lax.broadcasted_iota(jnp.int32, (1, band), 1)
    for blk in range(n_blocks):
        r0 = blk * n_rows
        b0 = blk * (CHUNK if not cached else band)
        for j in range(N_KV_HEADS):
            qg = q_groups[j][r0:r0 + n_rows]
            lhs = jnp.concatenate([qg * r.hmask[g:g + 1, :] for g in range(q_per_kv)],
                                  axis=0).astype(BF16)
            s = _dot_nt(lhs, r.kd[j, b0:b0 + band, :])
            if not cached and blk < WINDOW // CHUNK:
                first_valid = jnp.where(first, (WINDOW // CHUNK - blk) * CHUNK, 0)
                s = jnp.where(key_idx >= first_valid, s, NEG_INF)
            sink = jnp.concatenate(
                [jnp.full((n_rows, 1), r.sink[layer, j * q_per_kv + g], F32) for g in range(q_per_kv)], axis=0)
            m = jnp.maximum(jnp.max(s, axis=-1, keepdims=True), sink)
            pexp = jnp.exp(s - m)
            o2 = _dot(pexp.astype(BF16), r.vd[j, b0:b0 + band, :])
            o = o2[:, 0:LANES] / (o2[:, LANES:group_w] + jnp.exp(sink - m))
            for p in range(q_per_kv // 2):
                lo_rows = o[(2 * p) * n_rows:(2 * p + 1) * n_rows]
                hi_rows = o[(2 * p + 1) * n_rows:(2 * p + 2) * n_rows]
                col = (j * (q_per_kv // 2) + p) * LANES
                r.ao[r0:r0 + n_rows, col:col + LANES] = jnp.where(low_half, lo_rows, hi_rows).astype(BF16)
    branch_b = _dot(r.ao[...], r.wao[...])

    gates = _dot(r.h[...], r.wgate[...])
    merged = _sigmoid(gates[:, 0:d]) * branch_a + _sigmoid(gates[:, d:2 * d]) * branch_b
    out = _dot(merged.astype(BF16), r.wout[...])
    r.xo[...] = x3 + mod[:, :, 2 * d:3 * d] * out.reshape(nb, tl, d)


def _mixer(x, mod, w, sink, shared, tables, s0_re, s0_im, past_k, past_v, *, layer, nb, tl, cached):
    b, l, d = x.shape
    n_tiles = l // tl
    n_steps = (b // nb) * n_tiles
    tm = nb * tl
    ssm_w = w["dskip"].shape[-1]
    n_state = w["pwr"].shape[-1]
    if cached:
        nch, t_steps = nb, tl
        kr = tl
        band_rows = (nb, WINDOW + tl, LANES)
        dup_rows = nb * (WINDOW + tl)
    else:
        nch, t_steps = SUBLANES, tl // SUBLANES
        kr = WINDOW
        band_rows = (WINDOW + tm, LANES)
        dup_rows = WINDOW + tm
    n_s0 = s0_re.shape[2]

    def tile_map(i):
        return (i // n_tiles, i % n_tiles, 0)

    def layer_batch_map(i):
        return (layer, i // n_tiles, 0, 0)

    def batch_map(i):
        return (i // n_tiles, 0, 0)

    pos_spec = pl.BlockSpec((1, tl, LANES), lambda i: (0, i % n_tiles, 0))
    state_in_spec = pl.BlockSpec((None, 1, n_s0, n_state), layer_batch_map)
    state_out_spec = pl.BlockSpec((1, n_s0, n_state), batch_map)
    in_specs = (
        [pl.BlockSpec((nb, tl, d), tile_map),
         pl.BlockSpec((None, nb, 1, N_ADA * d), layer_batch_map)]
        + [_layer_block(w[n], layer) for n in MIXER_LAYER_WEIGHTS]
        + [pl.BlockSpec(memory_space=pltpu.SMEM)]
        + [_whole(shared[n]) for n in MIXER_SHARED]
        + [pos_spec] * len(MIXER_TABLES)
        + [state_in_spec, state_in_spec,
           pl.BlockSpec((None, nb) + past_k.shape[2:], layer_batch_map),
           pl.BlockSpec((None, nb) + past_v.shape[2:], layer_batch_map)]
    )
    out_specs = [
        pl.BlockSpec((nb, tl, d), tile_map),
        pl.BlockSpec((nb, kr, LANES), batch_map),
        pl.BlockSpec((nb, kr, LANES), batch_map),
        state_out_spec, state_out_spec,
    ]
    out_shape = [
        jax.ShapeDtypeStruct((b, l, d), F32),
        jax.ShapeDtypeStruct((b, kr, LANES), F32),
        jax.ShapeDtypeStruct((b, kr, LANES), F32),
        jax.ShapeDtypeStruct(s0_re.shape[1:], F32),
        jax.ShapeDtypeStruct(s0_im.shape[1:], F32),
    ]
    scratch = [
        pltpu.VMEM((tm, d), BF16),
        pltpu.VMEM((ssm_w // LANES, tm, LANES), F32),
        pltpu.VMEM((tm, ssm_w), F32),
        pltpu.VMEM((tm, n_state), F32),
        pltpu.VMEM((tm, n_state), F32),
        pltpu.VMEM((ssm_w // LANES, tm, LANES), F32),
        pltpu.VMEM((nch, n_state), F32),
        pltpu.VMEM((nch, n_state), F32),
        pltpu.VMEM((1, n_state), F32),
        pltpu.VMEM((1, n_state), F32),
        pltpu.VMEM(band_rows, F32),
        pltpu.VMEM(band_rows, F32),
        pltpu.VMEM((N_KV_HEADS, dup_rows, 2 * LANES), BF16),
        pltpu.VMEM((N_KV_HEADS, dup_rows, 2 * LANES), BF16),
        pltpu.VMEM((tm, N_HEADS * HEAD_DIM), BF16),
    ]
    kern = functools.partial(_mixer_kernel, nb=nb, tl=tl, nch=nch, t_steps=t_steps,
                             n_tiles=n_tiles, layer=layer, cached=cached)
    return pl.pallas_call(
        kern,
        grid=(n_steps,),
        in_specs=in_specs,
        out_specs=out_specs,
        out_shape=out_shape,
        scratch_shapes=scratch,
        compiler_params=pltpu.CompilerParams(dimension_semantics=("arbitrary",),
                                             vmem_limit_bytes=VMEM_LIMIT_BYTES),
        name="mixer_cached" if cached else "mixer_banded",
    )(x, mod, *[w[n] for n in MIXER_LAYER_WEIGHTS], sink, *[shared[n] for n in MIXER_SHARED], *tables,
      s0_re, s0_im, past_k, past_v)


FFN_LAYER_WEIGHTS = ("g2", "wup", "cw", "cb", "wdn")


def _ffn_kernel(*refs, nb, tl, n_tiles, col_w, final):
    r = _named(("x", "mod") + FFN_LAYER_WEIGHTS + ("fg", "prev", "xo", "cs", "ub", "carry"), refs)
    tm = nb * tl
    d = r.x.shape[-1]
    d_ff = r.wdn.shape[0]
    pad = SUBLANES
    first = pl.program_id(0) % n_tiles == 0

    @pl.when(first)
    def _():
        r.carry[...] = r.prev[...]

    x3 = r.x[...]
    mod = r.mod[...]
    h3 = _rms_modulate(x3, r.g2[...], mod[:, :, 3 * d:4 * d], mod[:, :, 4 * d:5 * d])
    hb = h3.reshape(tm, d).astype(BF16)

    def conv_cols(c0):
        up = _dot(hb, r.wup[:, c0:c0 + col_w]).reshape(nb, tl, col_w)
        r.ub[:, pad - 2:pad, :] = r.carry[:, :, c0:c0 + col_w]
        r.ub[:, pad:pad + tl, :] = up
        r.carry[:, :, c0:c0 + col_w] = r.ub[:, pad + tl - 2:pad + tl, :]
        w = r.cw[:, c0:c0 + col_w]
        out = r.cb[:, c0:c0 + col_w][None] + r.ub[:, pad - 2:pad - 2 + tl, :] * w[0:1][None]
        out = out + r.ub[:, pad - 1:pad - 1 + tl, :] * w[1:2][None]
        out = out + up * w[2:3][None]
        return out.reshape(tm, col_w)

    acc = jnp.zeros((tm, d), F32)
    for c0 in range(0, d_ff, col_w):
        val = conv_cols(c0)
        gate = conv_cols(d_ff + c0)
        act = (gate * _sigmoid(gate) * val).astype(BF16)
        acc = acc + _dot(act, r.wdn[c0:c0 + col_w, :])
    r.cs[...] = r.carry[...]
    xn = x3 + mod[:, :, 5 * d:6 * d] * acc.reshape(nb, tl, d)
    if final:
        xn = xn * lax.rsqrt(jnp.mean(xn * xn, axis=-1, keepdims=True) + RMS_EPS) * r.fg[...][None]
    r.xo[...] = xn


def _ffn_col_width(d_ff):
    assert d_ff % (FFN_COLUMN_BLOCKS * LANES) == 0, d_ff
    return d_ff // FFN_COLUMN_BLOCKS


def _ffn(x, mod, w, final_g, conv_prev, *, layer, nb, tl, final):
    b, l, d = x.shape
    n_tiles = l // tl
    n_steps = (b // nb) * n_tiles
    c = w["wup"].shape[-1]
    col_w = _ffn_col_width(w["wdn"].shape[1])

    def tile_map(i):
        return (i // n_tiles, i % n_tiles, 0)

    def layer_batch_map(i):
        return (layer, i // n_tiles, 0, 0)

    kern = functools.partial(_ffn_kernel, nb=nb, tl=tl, n_tiles=n_tiles, col_w=col_w, final=final)
    return pl.pallas_call(
        kern,
        grid=(n_steps,),
        in_specs=[pl.BlockSpec((nb, tl, d), tile_map),
                  pl.BlockSpec((None, nb, 1, N_ADA * d), layer_batch_map)]
        + [_layer_block(w[n], layer) for n in FFN_LAYER_WEIGHTS]
        + [_whole(final_g),
           pl.BlockSpec((None, nb, CONV_WIDTH - 1, c), layer_batch_map)],
        out_specs=[
            pl.BlockSpec((nb, tl, d), tile_map),
            pl.BlockSpec((nb, CONV_WIDTH - 1, c), lambda i: (i // n_tiles, 0, 0)),
        ],
        out_shape=[
            jax.ShapeDtypeStruct((b, l, d), F32),
            jax.ShapeDtypeStruct((b, CONV_WIDTH - 1, c), F32),
        ],
        scratch_shapes=[
            pltpu.VMEM((nb, SUBLANES + tl, col_w), F32),
            pltpu.VMEM((nb, CONV_WIDTH - 1, c), F32),
        ],
        compiler_params=pltpu.CompilerParams(dimension_semantics=("arbitrary",),
                                             vmem_limit_bytes=VMEM_LIMIT_BYTES),
        name="conv_ffn_final" if final else "conv_ffn",
    )(x, mod, *[w[n] for n in FFN_LAYER_WEIGHTS], final_g, conv_prev)


def _rotary_tables(pos):
    half = ROPE_DIM // 2
    inv_freq = ROPE_THETA ** (-(jnp.arange(half, dtype=F32) * 2.0) / ROPE_DIM)
    ang = pos.astype(F32)[:, None] * inv_freq[None, :]
    cos = jnp.cos(ang)
    sin = jnp.sin(ang)
    n = pos.shape[0]
    rest = HEAD_DIM - ROPE_DIM
    cos_h = jnp.concatenate([cos, cos, jnp.ones((n, rest), F32)], axis=1)
    sina_h = jnp.concatenate([-sin, jnp.zeros((n, half + rest), F32)], axis=1)
    sinb_h = jnp.concatenate([jnp.zeros((n, half), F32), sin, jnp.zeros((n, rest), F32)], axis=1)
    reps = LANES // HEAD_DIM
    key_tabs = tuple(jnp.tile(t, (1, reps))[None] for t in (cos_h, sina_h, sinb_h))
    q_per_kv = N_HEADS // N_KV_HEADS
    n_rot = q_per_kv * half
    cos_q = jnp.concatenate([jnp.tile(cos, (1, q_per_kv)), jnp.ones((n, LANES - n_rot), F32)], axis=1)
    sin_q = jnp.concatenate([jnp.tile(sin, (1, q_per_kv)), jnp.zeros((n, LANES - n_rot), F32)], axis=1)
    return key_tabs + (cos_q[None], sin_q[None])


def _group_layout():
    q_per_kv = N_HEADS // N_KV_HEADS
    half = ROPE_DIM // 2
    rest = HEAD_DIM - ROPE_DIM
    n_rot = q_per_kv * half
    slots, dims = [], []
    for col in range(2):
        for lane in range(LANES):
            if lane < n_rot:
                slots.append(lane // half)
                dims.append(lane % half + half * col)
            else:
                r = lane - n_rot
                slots.append(col * (q_per_kv // 2) + r // rest)
                dims.append(ROPE_DIM + r % rest)
    return slots, dims


def _attention_constants():
    q_per_kv = N_HEADS // N_KV_HEADS
    kv_w = N_KV_HEADS * HEAD_DIM
    slots, dims = _group_layout()
    group_w = len(slots)
    q_perm = [(j * q_per_kv + slots[n]) * HEAD_DIM + dims[n] for j in range(N_KV_HEADS) for n in range(group_w)]
    q_runs, start = [], 0
    for n in range(1, len(q_perm) + 1):
        if n == len(q_perm) or q_perm[n] != q_perm[n - 1] + 1:
            q_runs.append((q_perm[start], q_perm[n - 1] + 1))
            start = n
    dims_a = jnp.asarray(dims)
    slots_a = jnp.asarray(slots)
    rows = jnp.arange(kv_w)[:, None]
    gather_k = jnp.concatenate(
        [(rows == j * HEAD_DIM + dims_a[None, :]) for j in range(N_KV_HEADS)], axis=1).astype(BF16)
    pair_dim = jnp.arange(LANES) % HEAD_DIM
    gather_v = jnp.concatenate(
        [(rows == j * HEAD_DIM + pair_dim[None, :]) for j in range(N_KV_HEADS)], axis=1).astype(BF16)
    head_mask = (slots_a[None, :] == jnp.arange(q_per_kv)[:, None]).astype(F32)
    return q_runs, dict(gk=gather_k, gv=gather_v, hmask=head_mask)


def kernel(x_prompt, x_sample, cache_k, cache_v, state_ssm_re, state_ssm_im, state_conv, c_prompt, c_sample, ada_w, ada_b, norm1_g, norm2_g, w_in, ssm_a_re, ssm_a_im, ssm_log_dt, ssm_b_re, ssm_b_im, ssm_c_re, ssm_c_im, ssm_d, w_glu, attn_sink, w_attn_o, w_out, ffn_w_up, ffn_conv_w, ffn_conv_b, ffn_w_down, final_g):
    bp, lp, d = x_prompt.shape
    bs, ls, _ = x_sample.shape
    depth = w_in.shape[0]
    groups, n_p = ssm_a_re.shape[1], ssm_a_re.shape[2]
    n_state = groups * n_p
    ssm_w = ssm_d.shape[-1]
    attn_w = N_HEADS * HEAD_DIM
    kv_w = N_KV_HEADS * HEAD_DIM
    c_ff = ffn_w_up.shape[-1]
    n_qkv = ssm_w + attn_w + 2 * kv_w
    tl_p = min(PROMPT_TILE, lp)
    n_pow = max(tl_p // SUBLANES, ls)

    mod = _ada_modulation(jnp.concatenate([c_prompt, c_sample], axis=0), ada_w, ada_b)
    mod_p = mod[:, :bp].reshape(depth, bp, 1, N_ADA * d)
    mod_s = mod[:, bp:].reshape(depth, bs, 1, N_ADA * d)
    bbar_re, bbar_im, pw_re, pw_im = _discretise(ssm_a_re, ssm_a_im, ssm_log_dt, ssm_b_re, ssm_b_im, n_pow)
    gpb = MXU_DIM // SSM_GROUP
    q_runs, attn_shared = _attention_constants()

    w_mix = dict(
        g1=norm1_g[:, None, :],
        win=jnp.concatenate([w_in[:, :, :ssm_w]] + [w_in[:, :, ssm_w + a:ssm_w + b] for a, b in q_runs]
                            + [w_in[:, :, ssm_w + attn_w:n_qkv]], axis=2).astype(BF16),
        wgate=w_in[:, :, n_qkv:].astype(BF16),
        bre=_block_diag_in(bbar_re, gpb),
        bim=_block_diag_in(bbar_im, gpb),
        cre=_block_diag_out(ssm_c_re, gpb),
        cim=_block_diag_out(ssm_c_im, gpb),
        pwr=pw_re.reshape(depth, n_pow, n_state),
        pwi=pw_im.reshape(depth, n_pow, n_state),
        dskip=ssm_d[:, None, :],
        wglu=w_glu.astype(BF16),
        wao=w_attn_o.astype(BF16),
        wout=w_out.astype(BF16),
    )
    w_ffn = dict(
        g2=norm2_g[:, None, :],
        wup=ffn_w_up.astype(BF16),
        cw=ffn_conv_w,
        cb=ffn_conv_b[:, None, :],
        wdn=ffn_w_down.astype(BF16),
    )
    fg = final_g[None]

    tabs_p = _rotary_tables(jnp.arange(lp))
    tabs_s = _rotary_tables(PAST_LEN + jnp.arange(ls))
    zero_state = jnp.zeros((depth, bp, 1, n_state), F32)
    zero_conv = jnp.zeros((depth, bp, CONV_WIDTH - 1, c_ff), F32)
    zero_past = jnp.zeros((depth, bp, SUBLANES, LANES), F32)
    s0_re = state_ssm_re.reshape(depth, 1, bs, n_state)
    s0_im = state_ssm_im.reshape(depth, 1, bs, n_state)
    past_k = cache_k.reshape(depth, bs, WINDOW, kv_w)
    past_v = cache_v.reshape(depth, bs, WINDOW, kv_w)

    xp, xs = x_prompt, x_sample
    outs_p = [[] for _ in range(5)]
    outs_s = [[] for _ in range(5)]
    for i in range(depth):
        last = i == depth - 1
        xp, nk, nv, sr, si = _mixer(xp, mod_p, w_mix, attn_sink, attn_shared, tabs_p, zero_state, zero_state,
                                    zero_past, zero_past, layer=i, nb=1, tl=tl_p, cached=False)
        xp, cs = _ffn(xp, mod_p, w_ffn, fg, zero_conv, layer=i, nb=1, tl=tl_p, final=last)
        for lst, val in zip(outs_p, (
                nk.reshape(bp, WINDOW, N_KV_HEADS, HEAD_DIM), nv.reshape(bp, WINDOW, N_KV_HEADS, HEAD_DIM),
                sr.reshape(bp, groups, n_p), si.reshape(bp, groups, n_p), cs)):
            lst.append(val)

        xs, nk, nv, sr, si = _mixer(xs, mod_s, w_mix, attn_sink, attn_shared, tabs_s, s0_re, s0_im,
                                    past_k, past_v, layer=i, nb=bs, tl=ls, cached=True)
        xs, cs = _ffn(xs, mod_s, w_ffn, fg, state_conv, layer=i, nb=bs, tl=ls, final=last)
        for lst, val in zip(outs_s, (
                nk.reshape(bs, ls, N_KV_HEADS, HEAD_DIM), nv.reshape(bs, ls, N_KV_HEADS, HEAD_DIM),
                sr.reshape(bs, groups, n_p), si.reshape(bs, groups, n_p), cs)):
            lst.append(val)

    return (xp, xs) + tuple(jnp.stack(o, axis=0) for o in outs_p) + tuple(jnp.stack(o, axis=0) for o in outs_s)
```

```python
import functools
import math
import types

import jax
import jax.numpy as jnp
from jax import lax
from jax.experimental import pallas as pl
from jax.experimental.pallas import tpu as pltpu

CHUNK = 64
SSM_GROUP = 16
SSM_STATE = 64
N_HEADS = 8
N_KV_HEADS = 2
HEAD_DIM = 64
WINDOW = 128
ROPE_DIM = HEAD_DIM // 4
ROPE_THETA = 500000.0
CONV_WIDTH = 3
N_ADA = 6
RMS_EPS = 1e-6
NEG_INF = -1e30
PAST_LEN = 2048

LANES = 128
SUBLANES = 8
MXU_DIM = 256
VMEM_LIMIT_BYTES = 58 * 1024 * 1024

PROMPT_TILE = 512
SCAN_LANES = 512
FFN_COLUMN_BLOCKS = 1
BF16 = jnp.bfloat16
F32 = jnp.float32


def _dot(a, b):
    return jnp.dot(a, b, preferred_element_type=F32)


def _dot_nt(a, b):
    return lax.dot_general(a, b, (((1,), (1,)), ((), ())), preferred_element_type=F32)


def _sigmoid(x):
    return 0.5 * jnp.tanh(0.5 * x) + 0.5


def _gelu_tanh(x):
    c = math.sqrt(2.0 / math.pi)
    return 0.5 * x * (1.0 + jnp.tanh(c * (x + 0.044715 * (x * x * x))))


def _rms_modulate(x3, g_row, shift3, scale3):
    y = x3 * lax.rsqrt(jnp.mean(x3 * x3, axis=-1, keepdims=True) + RMS_EPS)
    y = y * g_row[None]
    return y * (1.0 + scale3) + shift3


def _named(names, refs):
    assert len(names) == len(refs), (len(names), len(refs))
    return types.SimpleNamespace(**dict(zip(names, refs)))


def _layer_block(a, layer):
    zeros = (0,) * (a.ndim - 1)
    return pl.BlockSpec((None,) + a.shape[1:], lambda i: (layer,) + zeros, pipeline_mode=pl.Buffered(1))


def _whole(a):
    zeros = (0,) * a.ndim
    return pl.BlockSpec(a.shape, lambda i: zeros, pipeline_mode=pl.Buffered(1))


def _ada_kernel(c_ref, w_ref, b_ref, o_ref):
    c = c_ref[...]
    a = (c * _sigmoid(c)).astype(BF16)
    o_ref[0] = _dot(a, w_ref[0].astype(BF16)) + b_ref[0]


def _ada_modulation(c_all, ada_w, ada_b):
    depth, d, e = ada_w.shape
    nb = c_all.shape[0]
    col = d
    return pl.pallas_call(
        _ada_kernel,
        grid=(depth, e // col),
        in_specs=[
            pl.BlockSpec((nb, d), lambda l, j: (0, 0)),
            pl.BlockSpec((1, d, col), lambda l, j: (l, 0, j)),
            pl.BlockSpec((1, 1, col), lambda l, j: (l, 0, j)),
        ],
        out_specs=pl.BlockSpec((1, nb, col), lambda l, j: (l, 0, j)),
        out_shape=jax.ShapeDtypeStruct((depth, nb, e), F32),
        name="ada_modulation",
    )(c_all, ada_w, ada_b.reshape(depth, 1, e))


def _discretise_kernel(are_ref, aim_ref, ldt_ref, bre_ref, bim_ref,
                       bbr_ref, bbi_ref, pwr_ref, pwi_ref, *, n_pow):
    a_re = are_ref[0]
    a_im = aim_ref[0]
    dt = jnp.exp(ldt_ref[0])
    mag = jnp.exp(a_re * dt)
    abar_re = mag * jnp.cos(a_im * dt)
    abar_im = mag * jnp.sin(a_im * dt)
    nr = abar_re - 1.0
    ni = abar_im
    den = a_re * a_re + a_im * a_im
    fr = (nr * a_re + ni * a_im) / den
    fi = (ni * a_re - nr * a_im) / den
    b_re = bre_ref[0]
    b_im = bim_ref[0]
    bbr_ref[0] = fr[:, None, :] * b_re - fi[:, None, :] * b_im
    bbi_ref[0] = fr[:, None, :] * b_im + fi[:, None, :] * b_re
    pr, pi = abar_re, abar_im
    pwr_ref[0, 0] = pr
    pwi_ref[0, 0] = pi
    for t in range(1, n_pow):
        pr, pi = abar_re * pr - abar_im * pi, abar_re * pi + abar_im * pr
        pwr_ref[0, t] = pr
        pwi_ref[0, t] = pi


def _discretise(a_re, a_im, log_dt, b_re, b_im, n_pow):
    depth, g, p = a_re.shape
    gc = b_re.shape[-1]
    b_re_t = jnp.swapaxes(b_re, 2, 3)
    b_im_t = jnp.swapaxes(b_im, 2, 3)
    spec_a = pl.BlockSpec((1, g, p), lambda l: (l, 0, 0))
    spec_b = pl.BlockSpec((1, g, gc, p), lambda l: (l, 0, 0, 0))
    spec_p = pl.BlockSpec((1, n_pow, g, p), lambda l: (l, 0, 0, 0))
    return pl.pallas_call(
        functools.partial(_discretise_kernel, n_pow=n_pow),
        grid=(depth,),
        in_specs=[spec_a, spec_a, pl.BlockSpec((1, g, 1), lambda l: (l, 0, 0)), spec_b, spec_b],
        out_specs=[spec_b, spec_b, spec_p, spec_p],
        out_shape=[jax.ShapeDtypeStruct((depth, g, gc, p), F32)] * 2
        + [jax.ShapeDtypeStruct((depth, n_pow, g, p), F32)] * 2,
        name="s5_discretise",
    )(a_re, a_im, log_dt.reshape(depth, g, 1), b_re_t, b_im_t)


def _block_diag(m, groups_per_block):
    depth, g, rows, cols = m.shape
    nb = g // groups_per_block
    m = m.reshape(depth, nb, groups_per_block, rows, cols)
    padded = [jnp.pad(m[:, :, j], ((0, 0), (0, 0), (0, 0), (j * cols, (groups_per_block - 1 - j) * cols)))
              for j in range(groups_per_block)]
    return jnp.stack(padded, axis=2).reshape(depth, nb, groups_per_block * rows, groups_per_block * cols)


def _block_diag_in(bbar, groups_per_block):
    return _block_diag(bbar.astype(BF16), groups_per_block)


def _block_diag_out(c, groups_per_block):
    return _block_diag(jnp.swapaxes(c, 2, 3).astype(BF16), groups_per_block)


MIXER_LAYER_WEIGHTS = ("g1", "win", "wgate", "bre", "bim", "cre", "cim", "pwr", "pwi", "dskip", "wglu",
                       "wao", "wout")
MIXER_SHARED = ("gk", "gv", "hmask")
MIXER_TABLES = ("cos", "sina", "sinb", "cq", "sq")
MIXER_STATE_IN = ("s0r", "s0i", "pk", "pv")
MIXER_OUT = ("xo", "ko", "vo", "sro", "sio")
MIXER_SCRATCH = ("h", "u", "up", "xr", "xi", "yp", "sinr", "sini", "carr", "cari", "kb", "vb", "kd", "vd", "ao")


def _mixer_kernel(*refs, nb, tl, nch, t_steps, n_tiles, layer, cached):
    r = _named(("x", "mod") + MIXER_LAYER_WEIGHTS + ("sink",) + MIXER_SHARED + MIXER_TABLES + MIXER_STATE_IN
               + MIXER_OUT + MIXER_SCRATCH, refs)
    tm = nb * tl
    d = r.x.shape[-1]
    ssm_w = r.up.shape[-1]
    n_state = r.xr.shape[-1]
    attn_w = N_HEADS * HEAD_DIM
    kv_w = N_KV_HEADS * HEAD_DIM
    first = pl.program_id(0) % n_tiles == 0

    x3 = r.x[...]
    mod = r.mod[...]
    h3 = _rms_modulate(x3, r.g1[...], mod[:, :, 0:d], mod[:, :, d:2 * d])
    hb = h3.reshape(tm, d).astype(BF16)
    r.h[...] = hb
    proj = _dot(hb, r.win[...])
    u = proj[:, 0:ssm_w]
    for c in range(ssm_w // LANES):
        r.u[c] = u[:, c * LANES:(c + 1) * LANES]
    q = proj[:, ssm_w:ssm_w + attn_w]
    k = proj[:, ssm_w + attn_w:ssm_w + attn_w + kv_w]
    v = proj[:, ssm_w + attn_w + kv_w:ssm_w + attn_w + 2 * kv_w]

    for t in range(t_steps):
        for c in range(ssm_w // LANES):
            r.up[t * nch:(t + 1) * nch, c * LANES:(c + 1) * LANES] = r.u[c, pl.ds(t, nch, stride=t_steps), :]
    n_in_blocks = r.bre.shape[0]
    kin = ssm_w // n_in_blocks
    nin = n_state // n_in_blocks
    for j in range(n_in_blocks):
        ub = r.up[:, j * kin:(j + 1) * kin].astype(BF16)
        r.xr[:, j * nin:(j + 1) * nin] = _dot(ub, r.bre[j])
        r.xi[:, j * nin:(j + 1) * nin] = _dot(ub, r.bim[j])

    def scan_block(lo, s_r, s_i, store):
        a_r = jnp.broadcast_to(r.pwr[0:1, lo:lo + SCAN_LANES], (nch, SCAN_LANES))
        a_i = jnp.broadcast_to(r.pwi[0:1, lo:lo + SCAN_LANES], (nch, SCAN_LANES))
        for t in range(t_steps):
            rows = slice(t * nch, (t + 1) * nch)
            in_r = r.xr[rows, lo:lo + SCAN_LANES]
            in_i = r.xi[rows, lo:lo + SCAN_LANES]
            if s_r is None:
                s_r, s_i = in_r, in_i
            else:
                s_r, s_i = a_r * s_r - a_i * s_i + in_r, a_r * s_i + a_i * s_r + in_i
            if store:
                r.xr[rows, lo:lo + SCAN_LANES] = s_r
                r.xi[rows, lo:lo + SCAN_LANES] = s_i
        return s_r, s_i

    if not cached:
        @pl.when(first)
        def _():
            r.carr[...] = r.s0r[0]
            r.cari[...] = r.s0i[0]

        for lo in range(0, n_state, SCAN_LANES):
            f_r, f_i = scan_block(lo, None, None, store=False)
            at_r = r.pwr[t_steps - 1:t_steps, lo:lo + SCAN_LANES]
            at_i = r.pwi[t_steps - 1:t_steps, lo:lo + SCAN_LANES]
            c_r = r.carr[:, lo:lo + SCAN_LANES]
            c_i = r.cari[:, lo:lo + SCAN_LANES]
            for ch in range(nch):
                r.sinr[ch:ch + 1, lo:lo + SCAN_LANES] = c_r
                r.sini[ch:ch + 1, lo:lo + SCAN_LANES] = c_i
                c_r, c_i = (at_r * c_r - at_i * c_i + f_r[ch:ch + 1],
                            at_r * c_i + at_i * c_r + f_i[ch:ch + 1])
            r.carr[:, lo:lo + SCAN_LANES] = c_r
            r.cari[:, lo:lo + SCAN_LANES] = c_i
        r.sro[0] = r.carr[...]
        r.sio[0] = r.cari[...]
    else:
        r.sinr[...] = r.s0r[0]
        r.sini[...] = r.s0i[0]

    gates = _dot(r.h[...], r.wgate[...])

    for lo in range(0, n_state, SCAN_LANES):
        f_r, f_i = scan_block(lo, r.sinr[:, lo:lo + SCAN_LANES], r.sini[:, lo:lo + SCAN_LANES], store=True)
        if cached:
            r.sro[0, :, lo:lo + SCAN_LANES] = f_r
            r.sio[0, :, lo:lo + SCAN_LANES] = f_i

    n_out_blocks = r.cre.shape[0]
    kout = n_state // n_out_blocks
    nout = ssm_w // n_out_blocks
    for j in range(n_out_blocks):
        sr = r.xr[:, j * kout:(j + 1) * kout].astype(BF16)
        si = r.xi[:, j * kout:(j + 1) * kout].astype(BF16)
        yb = _dot(sr, r.cre[j]) - _dot(si, r.cim[j])
        for c in range(nout // LANES):
            r.yp[j * (nout // LANES) + c] = yb[:, c * LANES:(c + 1) * LANES]
    y = jnp.concatenate(
        [jnp.concatenate([r.yp[c, pl.ds(ch, t_steps, stride=nch), :] for ch in range(nch)], axis=0)
         for c in range(ssm_w // LANES)], axis=1)
    y = y + r.dskip[...] * u
    glu = _dot(_gelu_tanh(y).astype(BF16), r.wglu[...])
    branch_a = glu[:, 0:d] * _sigmoid(glu[:, d:2 * d])

    cos3 = r.cos[...]
    sina3 = r.sina[...]
    sinb3 = r.sinb[...]
    k_rolled_down = pltpu.roll(k, LANES - ROPE_DIM // 2, axis=1).reshape(nb, tl, LANES)
    k_rolled_up = pltpu.roll(k, ROPE_DIM // 2, axis=1).reshape(nb, tl, LANES)
    k3 = k.reshape(nb, tl, LANES) * cos3 + k_rolled_down * sina3 + k_rolled_up * sinb3
    v3 = v.reshape(nb, tl, LANES)
    kr = r.ko.shape[1]
    r.ko[...] = k3[:, tl - kr:, :]
    r.vo[...] = v3[:, tl - kr:, :]
    if not cached:
        @pl.when(first)
        def _():
            r.kb[0:WINDOW, :] = jnp.zeros((WINDOW, LANES), F32)
            r.vb[0:WINDOW, :] = jnp.zeros((WINDOW, LANES), F32)

        r.kb[WINDOW:WINDOW + tm, :] = k3.reshape(tm, LANES)
        r.vb[WINDOW:WINDOW + tm, :] = v
        kall = r.kb[...]
        vall = r.vb[...]
        band = WINDOW + CHUNK
        n_rows = CHUNK
        n_blocks = tm // CHUNK
    else:
        r.kb[:, 0:WINDOW, :] = r.pk[...]
        r.kb[:, WINDOW:WINDOW + tl, :] = k3
        r.vb[:, 0:WINDOW, :] = r.pv[...]
        r.vb[:, WINDOW:WINDOW + tl, :] = v3
        band = WINDOW + tl
        n_rows = tl
        n_blocks = nb
        kall = r.kb[...].reshape(nb * band, LANES)
        vall = r.vb[...].reshape(nb * band, LANES)
    group_w = 2 * LANES
    k_spread = _dot(kall.astype(BF16), r.gk[...]).astype(BF16)
    v_spread = _dot(vall.astype(BF16), r.gv[...]).astype(BF16)
    for j in range(N_KV_HEADS):
        r.kd[j] = k_spread[:, j * group_w:(j + 1) * group_w]
        r.vd[j, :, 0:LANES] = v_spread[:, j * LANES:(j + 1) * LANES]
        r.vd[j, :, LANES:group_w] = jnp.ones((kall.shape[0], LANES), BF16)
    if not cached:
        r.kb[0:WINDOW, :] = r.kb[tm:tm + WINDOW, :]
        r.vb[0:WINDOW, :] = r.vb[tm:tm + WINDOW, :]

    scale = HEAD_DIM ** -0.5
    cq3 = r.cq[...]
    sq3 = r.sq[...]
    q_groups = []
    for j in range(N_KV_HEADS):
        c0 = q[:, j * group_w:j * group_w + LANES].reshape(nb, tl, LANES)
        c1 = q[:, j * group_w + LANES:(j + 1) * group_w].reshape(nb, tl, LANES)
        r0c = (c0 * cq3 - c1 * sq3).reshape(tm, LANES)
        r1c = (c1 * cq3 + c0 * sq3).reshape(tm, LANES)
        q_groups.append(jnp.concatenate([r0c, r1c], axis=1) * scale)
    q_per_kv = N_HEADS // N_KV_HEADS
    lane = lax.broadcasted_iota(jnp.int32, (1, LANES), 1)
    low_half = lane < HEAD_DIM
    key_idx = lax.broadcasted_iota(jnp.int32, (1, band), 1)
    for blk in range(n_blocks):
        r0 = blk * n_rows
        b0 = blk * (CHUNK if not cached else band)
        for j in range(N_KV_HEADS):
            qg = q_groups[j][r0:r0 + n_rows]
            lhs = jnp.concatenate([qg * r.hmask[g:g + 1, :] for g in range(q_per_kv)],
                                  axis=0).astype(BF16)
            s = _dot_nt(lhs, r.kd[j, b0:b0 + band, :])
            if not cached and blk < WINDOW // CHUNK:
                first_valid = jnp.where(first, (WINDOW // CHUNK - blk) * CHUNK, 0)
                s = jnp.where(key_idx >= first_valid, s, NEG_INF)
            sink = jnp.concatenate(
                [jnp.full((n_rows, 1), r.sink[layer, j * q_per_kv + g], F32) for g in range(q_per_kv)], axis=0)
            m = jnp.maximum(jnp.max(s, axis=-1, keepdims=True), sink)
            pexp = jnp.exp(s - m)
            o2 = _dot(pexp.astype(BF16), r.vd[j, b0:b0 + band, :])
            num = o2[:, 0:LANES]
            den = o2[:, LANES:group_w] + jnp.exp(sink - m)
            for p in range(q_per_kv // 2):
                lo = slice((2 * p) * n_rows, (2 * p + 1) * n_rows)
                hi = slice((2 * p + 1) * n_rows, (2 * p + 2) * n_rows)
                o = jnp.where(low_half, num[lo], num[hi]) / jnp.where(low_half, den[lo], den[hi])
                col = (j * (q_per_kv // 2) + p) * LANES
                r.ao[r0:r0 + n_rows, col:col + LANES] = o.astype(BF16)
    branch_b = _dot(r.ao[...], r.wao[...])

    merged = _sigmoid(gates[:, 0:d]) * branch_a + _sigmoid(gates[:, d:2 * d]) * branch_b
    out = _dot(merged.astype(BF16), r.wout[...])
    r.xo[...] = x3 + mod[:, :, 2 * d:3 * d] * out.reshape(nb, tl, d)


def _mixer(x, mod, w, sink, shared, tables, s0_re, s0_im, past_k, past_v, *, layer, nb, tl, cached):
    b, l, d = x.shape
    n_tiles = l // tl
    n_steps = (b // nb) * n_tiles
    tm = nb * tl
    ssm_w = w["dskip"].shape[-1]
    n_state = w["pwr"].shape[-1]
    if cached:
        nch, t_steps = nb, tl
        kr = tl
        band_rows = (nb, WINDOW + tl, LANES)
        dup_rows = nb * (WINDOW + tl)
    else:
        nch, t_steps = SUBLANES, tl // SUBLANES
        kr = WINDOW
        band_rows = (WINDOW + tm, LANES)
        dup_rows = WINDOW + tm
    n_s0 = s0_re.shape[2]

    def tile_map(i):
        return (i // n_tiles, i % n_tiles, 0)

    def layer_batch_map(i):
        return (layer, i // n_tiles, 0, 0)

    def batch_map(i):
        return (i // n_tiles, 0, 0)

    pos_spec = pl.BlockSpec((1, tl, LANES), lambda i: (0, i % n_tiles, 0))
    state_in_spec = pl.BlockSpec((None, 1, n_s0, n_state), layer_batch_map)
    state_out_spec = pl.BlockSpec((1, n_s0, n_state), batch_map)
    in_specs = (
        [pl.BlockSpec((nb, tl, d), tile_map),
         pl.BlockSpec((None, nb, 1, N_ADA * d), layer_batch_map)]
        + [_layer_block(w[n], layer) for n in MIXER_LAYER_WEIGHTS]
        + [pl.BlockSpec(memory_space=pltpu.SMEM)]
        + [_whole(shared[n]) for n in MIXER_SHARED]
        + [pos_spec] * len(MIXER_TABLES)
        + [state_in_spec, state_in_spec,
           pl.BlockSpec((None, nb) + past_k.shape[2:], layer_batch_map),
           pl.BlockSpec((None, nb) + past_v.shape[2:], layer_batch_map)]
    )
    out_specs = [
        pl.BlockSpec((nb, tl, d), tile_map),
        pl.BlockSpec((nb, kr, LANES), batch_map),
        pl.BlockSpec((nb, kr, LANES), batch_map),
        state_out_spec, state_out_spec,
    ]
    out_shape = [
        jax.ShapeDtypeStruct((b, l, d), F32),
        jax.ShapeDtypeStruct((b, kr, LANES), F32),
        jax.ShapeDtypeStruct((b, kr, LANES), F32),
        jax.ShapeDtypeStruct(s0_re.shape[1:], F32),
        jax.ShapeDtypeStruct(s0_im.shape[1:], F32),
    ]
    scratch = [
        pltpu.VMEM((tm, d), BF16),
        pltpu.VMEM((ssm_w // LANES, tm, LANES), F32),
        pltpu.VMEM((tm, ssm_w), F32),
        pltpu.VMEM((tm, n_state), F32),
        pltpu.VMEM((tm, n_state), F32),
        pltpu.VMEM((ssm_w // LANES, tm, LANES), F32),
        pltpu.VMEM((nch, n_state), F32),
        pltpu.VMEM((nch, n_state), F32),
        pltpu.VMEM((1, n_state), F32),
        pltpu.VMEM((1, n_state), F32),
        pltpu.VMEM(band_rows, F32),
        pltpu.VMEM(band_rows, F32),
        pltpu.VMEM((N_KV_HEADS, dup_rows, 2 * LANES), BF16),
        pltpu.VMEM((N_KV_HEADS, dup_rows, 2 * LANES), BF16),
        pltpu.VMEM((tm, N_HEADS * HEAD_DIM), BF16),
    ]
    kern = functools.partial(_mixer_kernel, nb=nb, tl=tl, nch=nch, t_steps=t_steps,
                             n_tiles=n_tiles, layer=layer, cached=cached)
    return pl.pallas_call(
        kern,
        grid=(n_steps,),
        in_specs=in_specs,
        out_specs=out_specs,
        out_shape=out_shape,
        scratch_shapes=scratch,
        compiler_params=pltpu.CompilerParams(dimension_semantics=("arbitrary",),
                                             vmem_limit_bytes=VMEM_LIMIT_BYTES),
        name="mixer_cached" if cached else "mixer_banded",
    )(x, mod, *[w[n] for n in MIXER_LAYER_WEIGHTS], sink, *[shared[n] for n in MIXER_SHARED], *tables,
      s0_re, s0_im, past_k, past_v)


FFN_LAYER_WEIGHTS = ("g2", "wup", "cw", "cb", "wdn")


def _ffn_kernel(*refs, nb, tl, n_tiles, col_w, final):
    r = _named(("x", "mod") + FFN_LAYER_WEIGHTS + ("fg", "prev", "xo", "cs", "ub", "carry"), refs)
    tm = nb * tl
    d = r.x.shape[-1]
    d_ff = r.wdn.shape[0]
    pad = SUBLANES
    first = pl.program_id(0) % n_tiles == 0

    @pl.when(first)
    def _():
        r.carry[...] = r.prev[...]

    x3 = r.x[...]
    mod = r.mod[...]
    h3 = _rms_modulate(x3, r.g2[...], mod[:, :, 3 * d:4 * d], mod[:, :, 4 * d:5 * d])
    hb = h3.reshape(tm, d).astype(BF16)

    def conv_cols(c0):
        up = _dot(hb, r.wup[:, c0:c0 + col_w]).reshape(nb, tl, col_w)
        r.ub[:, pad - 2:pad, :] = r.carry[:, :, c0:c0 + col_w]
        r.ub[:, pad:pad + tl, :] = up
        r.carry[:, :, c0:c0 + col_w] = r.ub[:, pad + tl - 2:pad + tl, :]
        w = r.cw[:, c0:c0 + col_w]
        out = r.cb[:, c0:c0 + col_w][None] + r.ub[:, pad - 2:pad - 2 + tl, :] * w[0:1][None]
        out = out + r.ub[:, pad - 1:pad - 1 + tl, :] * w[1:2][None]
        out = out + up * w[2:3][None]
        return out.reshape(tm, col_w)

    acc = jnp.zeros((tm, d), F32)
    for c0 in range(0, d_ff, col_w):
        val = conv_cols(c0)
        gate = conv_cols(d_ff + c0)
        act = (gate * _sigmoid(gate) * val).astype(BF16)
        acc = acc + _dot(act, r.wdn[c0:c0 + col_w, :])
    r.cs[...] = r.carry[...]
    xn = x3 + mod[:, :, 5 * d:6 * d] * acc.reshape(nb, tl, d)
    if final:
        xn = xn * lax.rsqrt(jnp.mean(xn * xn, axis=-1, keepdims=True) + RMS_EPS) * r.fg[...][None]
    r.xo[...] = xn


def _ffn_col_width(d_ff):
    assert d_ff % (FFN_COLUMN_BLOCKS * LANES) == 0, d_ff
    return d_ff // FFN_COLUMN_BLOCKS


def _ffn(x, mod, w, final_g, conv_prev, *, layer, nb, tl, final):
    b, l, d = x.shape
    n_tiles = l // tl
    n_steps = (b // nb) * n_tiles
    c = w["wup"].shape[-1]
    col_w = _ffn_col_width(w["wdn"].shape[1])

    def tile_map(i):
        return (i // n_tiles, i % n_tiles, 0)

    def layer_batch_map(i):
        return (layer, i // n_tiles, 0, 0)

    kern = functools.partial(_ffn_kernel, nb=nb, tl=tl, n_tiles=n_tiles, col_w=col_w, final=final)
    return pl.pallas_call(
        kern,
        grid=(n_steps,),
        in_specs=[pl.BlockSpec((nb, tl, d), tile_map),
                  pl.BlockSpec((None, nb, 1, N_ADA * d), layer_batch_map)]
        + [_layer_block(w[n], layer) for n in FFN_LAYER_WEIGHTS]
        + [_whole(final_g),
           pl.BlockSpec((None, nb, CONV_WIDTH - 1, c), layer_batch_map)],
        out_specs=[
            pl.BlockSpec((nb, tl, d), tile_map),
            pl.BlockSpec((nb, CONV_WIDTH - 1, c), lambda i: (i // n_tiles, 0, 0)),
        ],
        out_shape=[
            jax.ShapeDtypeStruct((b, l, d), F32),
            jax.ShapeDtypeStruct((b, CONV_WIDTH - 1, c), F32),
        ],
        scratch_shapes=[
            pltpu.VMEM((nb, SUBLANES + tl, col_w), F32),
            pltpu.VMEM((nb, CONV_WIDTH - 1, c), F32),
        ],
        compiler_params=pltpu.CompilerParams(dimension_semantics=("arbitrary",),
                                             vmem_limit_bytes=VMEM_LIMIT_BYTES),
        name="conv_ffn_final" if final else "conv_ffn",
    )(x, mod, *[w[n] for n in FFN_LAYER_WEIGHTS], final_g, conv_prev)


def _rotary_tables(pos):
    half = ROPE_DIM // 2
    inv_freq = ROPE_THETA ** (-(jnp.arange(half, dtype=F32) * 2.0) / ROPE_DIM)
    ang = pos.astype(F32)[:, None] * inv_freq[None, :]
    cos = jnp.cos(ang)
    sin = jnp.sin(ang)
    n = pos.shape[0]
    rest = HEAD_DIM - ROPE_DIM
    cos_h = jnp.concatenate([cos, cos, jnp.ones((n, rest), F32)], axis=1)
    sina_h = jnp.concatenate([-sin, jnp.zeros((n, half + rest), F32)], axis=1)
    sinb_h = jnp.concatenate([jnp.zeros((n, half), F32), sin, jnp.zeros((n, rest), F32)], axis=1)
    reps = LANES // HEAD_DIM
    key_tabs = tuple(jnp.tile(t, (1, reps))[None] for t in (cos_h, sina_h, sinb_h))
    q_per_kv = N_HEADS // N_KV_HEADS
    n_rot = q_per_kv * half
    cos_q = jnp.concatenate([jnp.tile(cos, (1, q_per_kv)), jnp.ones((n, LANES - n_rot), F32)], axis=1)
    sin_q = jnp.concatenate([jnp.tile(sin, (1, q_per_kv)), jnp.zeros((n, LANES - n_rot), F32)], axis=1)
    return key_tabs + (cos_q[None], sin_q[None])


def _group_layout():
    q_per_kv = N_HEADS // N_KV_HEADS
    half = ROPE_DIM // 2
    rest = HEAD_DIM - ROPE_DIM
    n_rot = q_per_kv * half
    slots, dims = [], []
    for col in range(2):
        for lane in range(LANES):
            if lane < n_rot:
                slots.append(lane // half)
                dims.append(lane % half + half * col)
            else:
                r = lane - n_rot
                slots.append(col * (q_per_kv // 2) + r // rest)
                dims.append(ROPE_DIM + r % rest)
    return slots, dims


def _attention_constants():
    q_per_kv = N_HEADS // N_KV_HEADS
    kv_w = N_KV_HEADS * HEAD_DIM
    slots, dims = _group_layout()
    group_w = len(slots)
    q_perm = [(j * q_per_kv + slots[n]) * HEAD_DIM + dims[n] for j in range(N_KV_HEADS) for n in range(group_w)]
    q_runs, start = [], 0
    for n in range(1, len(q_perm) + 1):
        if n == len(q_perm) or q_perm[n] != q_perm[n - 1] + 1:
            q_runs.append((q_perm[start], q_perm[n - 1] + 1))
            start = n
    dims_a = jnp.asarray(dims)
    slots_a = jnp.asarray(slots)
    rows = jnp.arange(kv_w)[:, None]
    gather_k = jnp.concatenate(
        [(rows == j * HEAD_DIM + dims_a[None, :]) for j in range(N_KV_HEADS)], axis=1).astype(BF16)
    pair_dim = jnp.arange(LANES) % HEAD_DIM
    gather_v = jnp.concatenate(
        [(rows == j * HEAD_DIM + pair_dim[None, :]) for j in range(N_KV_HEADS)], axis=1).astype(BF16)
    head_mask = (slots_a[None, :] == jnp.arange(q_per_kv)[:, None]).astype(F32)
    return q_runs, dict(gk=gather_k, gv=gather_v, hmask=head_mask)


def kernel(x_prompt, x_sample, cache_k, cache_v, state_ssm_re, state_ssm_im, state_conv, c_prompt, c_sample, ada_w, ada_b, norm1_g, norm2_g, w_in, ssm_a_re, ssm_a_im, ssm_log_dt, ssm_b_re, ssm_b_im, ssm_c_re, ssm_c_im, ssm_d, w_glu, attn_sink, w_attn_o, w_out, ffn_w_up, ffn_conv_w, ffn_conv_b, ffn_w_down, final_g):
    bp, lp, d = x_prompt.shape
    bs, ls, _ = x_sample.shape
    depth = w_in.shape[0]
    groups, n_p = ssm_a_re.shape[1], ssm_a_re.shape[2]
    n_state = groups * n_p
    ssm_w = ssm_d.shape[-1]
    attn_w = N_HEADS * HEAD_DIM
    kv_w = N_KV_HEADS * HEAD_DIM
    c_ff = ffn_w_up.shape[-1]
    n_qkv = ssm_w + attn_w + 2 * kv_w
    tl_p = min(PROMPT_TILE, lp)
    n_pow = max(tl_p // SUBLANES, ls)

    mod = _ada_modulation(jnp.concatenate([c_prompt, c_sample], axis=0), ada_w, ada_b)
    mod_p = mod[:, :bp].reshape(depth, bp, 1, N_ADA * d)
    mod_s = mod[:, bp:].reshape(depth, bs, 1, N_ADA * d)
    bbar_re, bbar_im, pw_re, pw_im = _discretise(ssm_a_re, ssm_a_im, ssm_log_dt, ssm_b_re, ssm_b_im, n_pow)
    gpb = MXU_DIM // SSM_GROUP
    q_runs, attn_shared = _attention_constants()

    w_mix = dict(
        g1=norm1_g[:, None, :],
        win=jnp.concatenate([w_in[:, :, :ssm_w]] + [w_in[:, :, ssm_w + a:ssm_w + b] for a, b in q_runs]
                            + [w_in[:, :, ssm_w + attn_w:n_qkv]], axis=2).astype(BF16),
        wgate=w_in[:, :, n_qkv:].astype(BF16),
        bre=_block_diag_in(bbar_re, gpb),
        bim=_block_diag_in(bbar_im, gpb),
        cre=_block_diag_out(ssm_c_re, gpb),
        cim=_block_diag_out(ssm_c_im, gpb),
        pwr=pw_re.reshape(depth, n_pow, n_state),
        pwi=pw_im.reshape(depth, n_pow, n_state),
        dskip=ssm_d[:, None, :],
        wglu=w_glu.astype(BF16),
        wao=w_attn_o.astype(BF16),
        wout=w_out.astype(BF16),
    )
    w_ffn = dict(
        g2=norm2_g[:, None, :],
        wup=ffn_w_up.astype(BF16),
        cw=ffn_conv_w,
        cb=ffn_conv_b[:, None, :],
        wdn=ffn_w_down.astype(BF16),
    )
    fg = final_g[None]

    tabs_p = _rotary_tables(jnp.arange(lp))
    tabs_s = _rotary_tables(PAST_LEN + jnp.arange(ls))
    zero_state = jnp.zeros((depth, bp, 1, n_state), F32)
    zero_conv = jnp.zeros((depth, bp, CONV_WIDTH - 1, c_ff), F32)
    zero_past = jnp.zeros((depth, bp, SUBLANES, LANES), F32)
    s0_re = state_ssm_re.reshape(depth, 1, bs, n_state)
    s0_im = state_ssm_im.reshape(depth, 1, bs, n_state)
    past_k = cache_k.reshape(depth, bs, WINDOW, kv_w)
    past_v = cache_v.reshape(depth, bs, WINDOW, kv_w)

    xp, xs = x_prompt, x_sample
    outs_p = [[] for _ in range(5)]
    outs_s = [[] for _ in range(5)]
    for i in range(depth):
        last = i == depth - 1
        xp, nk, nv, sr, si = _mixer(xp, mod_p, w_mix, attn_sink, attn_shared, tabs_p, zero_state, zero_state,
                                    zero_past, zero_past, layer=i, nb=1, tl=tl_p, cached=False)
        xp, cs = _ffn(xp, mod_p, w_ffn, fg, zero_conv, layer=i, nb=1, tl=tl_p, final=last)
        for lst, val in zip(outs_p, (
                nk.reshape(bp, WINDOW, N_KV_HEADS, HEAD_DIM), nv.reshape(bp, WINDOW, N_KV_HEADS, HEAD_DIM),
                sr.reshape(bp, groups, n_p), si.reshape(bp, groups, n_p), cs)):
            lst.append(val)

        xs, nk, nv, sr, si = _mixer(xs, mod_s, w_mix, attn_sink, attn_shared, tabs_s, s0_re, s0_im,
                                    past_k, past_v, layer=i, nb=bs, tl=ls, cached=True)
        xs, cs = _ffn(xs, mod_s, w_ffn, fg, state_conv, layer=i, nb=bs, tl=ls, final=last)
        for lst, val in zip(outs_s, (
                nk.reshape(bs, ls, N_KV_HEADS, HEAD_DIM), nv.reshape(bs, ls, N_KV_HEADS, HEAD_DIM),
                sr.reshape(bs, groups, n_p), si.reshape(bs, groups, n_p), cs)):
            lst.append(val)

    return (xp, xs) + tuple(jnp.stack(o, axis=0) for o in outs_p) + tuple(jnp.stack(o, axis=0) for o in outs_s)
```

```python
import functools
import math
import types

import jax
import jax.numpy as jnp
from jax import lax
from jax.experimental import pallas as pl
from jax.experimental.pallas import tpu as pltpu

CHUNK = 64
SSM_GROUP = 16
SSM_STATE = 64
N_HEADS = 8
N_KV_HEADS = 2
HEAD_DIM = 64
WINDOW = 128
ROPE_DIM = HEAD_DIM // 4
ROPE_THETA = 500000.0
CONV_WIDTH = 3
N_ADA = 6
RMS_EPS = 1e-6
NEG_INF = -1e30
PAST_LEN = 2048

LANES = 128
SUBLANES = 8
MXU_DIM = 256
VMEM_LIMIT_BYTES = 58 * 1024 * 1024

PROMPT_TILE = 512
SCAN_LANES = 512
FFN_COLUMN_BLOCKS = 1
BF16 = jnp.bfloat16
F32 = jnp.float32


def _dot(a, b):
    return jnp.dot(a, b, preferred_element_type=F32)


def _dot_nt(a, b):
    return lax.dot_general(a, b, (((1,), (1,)), ((), ())), preferred_element_type=F32)


def _sigmoid(x):
    return 0.5 * jnp.tanh(0.5 * x) + 0.5


def _gelu_tanh(x):
    c = math.sqrt(2.0 / math.pi)
    return 0.5 * x * (1.0 + jnp.tanh(c * (x + 0.044715 * (x * x * x))))


def _rms_modulate(x3, g_row, shift3, scale3):
    y = x3 * lax.rsqrt(jnp.mean(x3 * x3, axis=-1, keepdims=True) + RMS_EPS)
    y = y * g_row[None]
    return y * (1.0 + scale3) + shift3


def _named(names, refs):
    assert len(names) == len(refs), (len(names), len(refs))
    return types.SimpleNamespace(**dict(zip(names, refs)))


def _layer_block(a, layer):
    zeros = (0,) * (a.ndim - 1)
    return pl.BlockSpec((None,) + a.shape[1:], lambda i: (layer,) + zeros, pipeline_mode=pl.Buffered(1))


def _whole(a):
    zeros = (0,) * a.ndim
    return pl.BlockSpec(a.shape, lambda i: zeros, pipeline_mode=pl.Buffered(1))


def _ada_kernel(c_ref, w_ref, b_ref, o_ref):
    c = c_ref[...]
    a = (c * _sigmoid(c)).astype(BF16)
    o_ref[0] = _dot(a, w_ref[0].astype(BF16)) + b_ref[0]


def _ada_modulation(c_all, ada_w, ada_b):
    depth, d, e = ada_w.shape
    nb = c_all.shape[0]
    col = d
    return pl.pallas_call(
        _ada_kernel,
        grid=(depth, e // col),
        in_specs=[
            pl.BlockSpec((nb, d), lambda l, j: (0, 0)),
            pl.BlockSpec((1, d, col), lambda l, j: (l, 0, j)),
            pl.BlockSpec((1, 1, col), lambda l, j: (l, 0, j)),
        ],
        out_specs=pl.BlockSpec((1, nb, col), lambda l, j: (l, 0, j)),
        out_shape=jax.ShapeDtypeStruct((depth, nb, e), F32),
        name="ada_modulation",
    )(c_all, ada_w, ada_b.reshape(depth, 1, e))


def _discretise_kernel(are_ref, aim_ref, ldt_ref, bre_ref, bim_ref,
                       bbr_ref, bbi_ref, pwr_ref, pwi_ref, *, n_pow):
    a_re = are_ref[0]
    a_im = aim_ref[0]
    dt = jnp.exp(ldt_ref[0])
    mag = jnp.exp(a_re * dt)
    abar_re = mag * jnp.cos(a_im * dt)
    abar_im = mag * jnp.sin(a_im * dt)
    nr = abar_re - 1.0
    ni = abar_im
    den = a_re * a_re + a_im * a_im
    fr = (nr * a_re + ni * a_im) / den
    fi = (ni * a_re - nr * a_im) / den
    b_re = bre_ref[0]
    b_im = bim_ref[0]
    bbr_ref[0] = fr[:, None, :] * b_re - fi[:, None, :] * b_im
    bbi_ref[0] = fr[:, None, :] * b_im + fi[:, None, :] * b_re
    pr, pi = abar_re, abar_im
    pwr_ref[0, 0] = pr
    pwi_ref[0, 0] = pi
    for t in range(1, n_pow):
        pr, pi = abar_re * pr - abar_im * pi, abar_re * pi + abar_im * pr
        pwr_ref[0, t] = pr
        pwi_ref[0, t] = pi


def _discretise(a_re, a_im, log_dt, b_re, b_im, n_pow):
    depth, g, p = a_re.shape
    gc = b_re.shape[-1]
    b_re_t = jnp.swapaxes(b_re, 2, 3)
    b_im_t = jnp.swapaxes(b_im, 2, 3)
    spec_a = pl.BlockSpec((1, g, p), lambda l: (l, 0, 0))
    spec_b = pl.BlockSpec((1, g, gc, p), lambda l: (l, 0, 0, 0))
    spec_p = pl.BlockSpec((1, n_pow, g, p), lambda l: (l, 0, 0, 0))
    return pl.pallas_call(
        functools.partial(_discretise_kernel, n_pow=n_pow),
        grid=(depth,),
        in_specs=[spec_a, spec_a, pl.BlockSpec((1, g, 1), lambda l: (l, 0, 0)), spec_b, spec_b],
        out_specs=[spec_b, spec_b, spec_p, spec_p],
        out_shape=[jax.ShapeDtypeStruct((depth, g, gc, p), F32)] * 2
        + [jax.ShapeDtypeStruct((depth, n_pow, g, p), F32)] * 2,
        name="s5_discretise",
    )(a_re, a_im, log_dt.reshape(depth, g, 1), b_re_t, b_im_t)


def _block_diag(m, groups_per_block):
    depth, g, rows, cols = m.shape
    nb = g // groups_per_block
    m = m.reshape(depth, nb, groups_per_block, rows, cols)
    padded = [jnp.pad(m[:, :, j], ((0, 0), (0, 0), (0, 0), (j * cols, (groups_per_block - 1 - j) * cols)))
              for j in range(groups_per_block)]
    return jnp.stack(padded, axis=2).reshape(depth, nb, groups_per_block * rows, groups_per_block * cols)


def _block_diag_in(bbar, groups_per_block):
    return _block_diag(bbar.astype(BF16), groups_per_block)


def _block_diag_out(c, groups_per_block):
    return _block_diag(jnp.swapaxes(c, 2, 3).astype(BF16), groups_per_block)


MIXER_LAYER_WEIGHTS = ("g1", "win", "wgate", "bre", "bim", "cre", "cim", "pwr", "pwi", "dskip", "wglu",
                       "wao", "wout")
MIXER_SHARED = ("gk", "gv", "hmask")
MIXER_TABLES = ("cos", "sina", "sinb", "cq", "sq")
MIXER_STATE_IN = ("s0r", "s0i", "pk", "pv")
MIXER_OUT = ("xo", "ko", "vo", "sro", "sio")
MIXER_SCRATCH = ("h", "u", "up", "xr", "xi", "yp", "sinr", "sini", "carr", "cari", "kb", "vb", "kd", "vd", "ao")


def _mixer_kernel(*refs, nb, tl, nch, t_steps, n_tiles, layer, cached):
    r = _named(("x", "mod") + MIXER_LAYER_WEIGHTS + ("sink",) + MIXER_SHARED + MIXER_TABLES + MIXER_STATE_IN
               + MIXER_OUT + MIXER_SCRATCH, refs)
    tm = nb * tl
    d = r.x.shape[-1]
    ssm_w = r.up.shape[-1]
    n_state = r.xr.shape[-1]
    attn_w = N_HEADS * HEAD_DIM
    kv_w = N_KV_HEADS * HEAD_DIM
    first = pl.program_id(0) % n_tiles == 0

    x3 = r.x[...]
    mod = r.mod[...]
    h3 = _rms_modulate(x3, r.g1[...], mod[:, :, 0:d], mod[:, :, d:2 * d])
    hb = h3.reshape(tm, d).astype(BF16)
    r.h[...] = hb
    proj = _dot(hb, r.win[...])
    u = proj[:, 0:ssm_w]
    for c in range(ssm_w // LANES):
        r.u[c] = u[:, c * LANES:(c + 1) * LANES]
    q = proj[:, ssm_w:ssm_w + attn_w]
    k = proj[:, ssm_w + attn_w:ssm_w + attn_w + kv_w]
    v = proj[:, ssm_w + attn_w + kv_w:ssm_w + attn_w + 2 * kv_w]

    for t in range(t_steps):
        for c in range(ssm_w // LANES):
            r.up[t * nch:(t + 1) * nch, c * LANES:(c + 1) * LANES] = r.u[c, pl.ds(t, nch, stride=t_steps), :]
    n_in_blocks = r.bre.shape[0]
    kin = ssm_w // n_in_blocks
    nin = n_state // n_in_blocks
    for j in range(n_in_blocks):
        ub = r.up[:, j * kin:(j + 1) * kin].astype(BF16)
        r.xr[:, j * nin:(j + 1) * nin] = _dot(ub, r.bre[j])
        r.xi[:, j * nin:(j + 1) * nin] = _dot(ub, r.bim[j])

    def scan_block(lo, s_r, s_i, store):
        a_r = jnp.broadcast_to(r.pwr[0:1, lo:lo + SCAN_LANES], (nch, SCAN_LANES))
        a_i = jnp.broadcast_to(r.pwi[0:1, lo:lo + SCAN_LANES], (nch, SCAN_LANES))
        for t in range(t_steps):
            rows = slice(t * nch, (t + 1) * nch)
            in_r = r.xr[rows, lo:lo + SCAN_LANES]
            in_i = r.xi[rows, lo:lo + SCAN_LANES]
            if s_r is None:
                s_r, s_i = in_r, in_i
            else:
                s_r, s_i = a_r * s_r - a_i * s_i + in_r, a_r * s_i + a_i * s_r + in_i
            if store:
                r.xr[rows, lo:lo + SCAN_LANES] = s_r
                r.xi[rows, lo:lo + SCAN_LANES] = s_i
        return s_r, s_i

    if not cached:
        @pl.when(first)
        def _():
            r.carr[...] = r.s0r[0]
            r.cari[...] = r.s0i[0]

        for lo in range(0, n_state, SCAN_LANES):
            f_r, f_i = scan_block(lo, None, None, store=False)
            at_r = r.pwr[t_steps - 1:t_steps, lo:lo + SCAN_LANES]
            at_i = r.pwi[t_steps - 1:t_steps, lo:lo + SCAN_LANES]
            c_r = r.carr[:, lo:lo + SCAN_LANES]
            c_i = r.cari[:, lo:lo + SCAN_LANES]
            for ch in range(nch):
                r.sinr[ch:ch + 1, lo:lo + SCAN_LANES] = c_r
                r.sini[ch:ch + 1, lo:lo + SCAN_LANES] = c_i
                c_r, c_i = (at_r * c_r - at_i * c_i + f_r[ch:ch + 1],
                            at_r * c_i + at_i * c_r + f_i[ch:ch + 1])
            r.carr[:, lo:lo + SCAN_LANES] = c_r
            r.cari[:, lo:lo + SCAN_LANES] = c_i
        r.sro[0] = r.carr[...]
        r.sio[0] = r.cari[...]
    else:
        r.sinr[...] = r.s0r[0]
        r.sini[...] = r.s0i[0]

    gates = _dot(r.h[...], r.wgate[...])

    for lo in range(0, n_state, SCAN_LANES):
        f_r, f_i = scan_block(lo, r.sinr[:, lo:lo + SCAN_LANES], r.sini[:, lo:lo + SCAN_LANES], store=True)
        if cached:
            r.sro[0, :, lo:lo + SCAN_LANES] = f_r
            r.sio[0, :, lo:lo + SCAN_LANES] = f_i

    n_out_blocks = r.cre.shape[0]
    kout = n_state // n_out_blocks
    nout = ssm_w // n_out_blocks
    for j in range(n_out_blocks):
        sr = r.xr[:, j * kout:(j + 1) * kout].astype(BF16)
        si = r.xi[:, j * kout:(j + 1) * kout].astype(BF16)
        yb = _dot(sr, r.cre[j]) - _dot(si, r.cim[j])
        for c in range(nout // LANES):
            r.yp[j * (nout // LANES) + c] = yb[:, c * LANES:(c + 1) * LANES]
    y = jnp.concatenate(
        [jnp.concatenate([r.yp[c, pl.ds(ch, t_steps, stride=nch), :] for ch in range(nch)], axis=0)
         for c in range(ssm_w // LANES)], axis=1)
    y = y + r.dskip[...] * u
    glu = _dot(_gelu_tanh(y).astype(BF16), r.wglu[...])
    branch_a = glu[:, 0:d] * _sigmoid(glu[:, d:2 * d])

    cos3 = r.cos[...]
    sina3 = r.sina[...]
    sinb3 = r.sinb[...]
    k_rolled_down = pltpu.roll(k, LANES - ROPE_DIM // 2, axis=1).reshape(nb, tl, LANES)
    k_rolled_up = pltpu.roll(k, ROPE_DIM // 2, axis=1).reshape(nb, tl, LANES)
    k3 = k.reshape(nb, tl, LANES) * cos3 + k_rolled_down * sina3 + k_rolled_up * sinb3
    v3 = v.reshape(nb, tl, LANES)
    kr = r.ko.shape[1]
    r.ko[...] = k3[:, tl - kr:, :]
    r.vo[...] = v3[:, tl - kr:, :]
    if not cached:
        @pl.when(first)
        def _():
            r.kb[0:WINDOW, :] = jnp.zeros((WINDOW, LANES), F32)
            r.vb[0:WINDOW, :] = jnp.zeros((WINDOW, LANES), F32)

        r.kb[WINDOW:WINDOW + tm, :] = k3.reshape(tm, LANES)
        r.vb[WINDOW:WINDOW + tm, :] = v
        kall = r.kb[...]
        vall = r.vb[...]
        band = WINDOW + CHUNK
        n_rows = CHUNK
        n_blocks = tm // CHUNK
    else:
        r.kb[:, 0:WINDOW, :] = r.pk[...]
        r.kb[:, WINDOW:WINDOW + tl, :] = k3
        r.vb[:, 0:WINDOW, :] = r.pv[...]
        r.vb[:, WINDOW:WINDOW + tl, :] = v3
        band = WINDOW + tl
        n_rows = tl
        n_blocks = nb
        kall = r.kb[...].reshape(nb * band, LANES)
        vall = r.vb[...].reshape(nb * band, LANES)
    group_w = 2 * LANES
    k_spread = _dot(kall.astype(BF16), r.gk[...]).astype(BF16)
    v_spread = _dot(vall.astype(BF16), r.gv[...]).astype(BF16)
    for j in range(N_KV_HEADS):
        r.kd[j] = k_spread[:, j * group_w:(j + 1) * group_w]
        r.vd[j, :, 0:LANES] = v_spread[:, j * LANES:(j + 1) * LANES]
        r.vd[j, :, LANES:group_w] = jnp.ones((kall.shape[0], LANES), BF16)
    if not cached:
        r.kb[0:WINDOW, :] = r.kb[tm:tm + WINDOW, :]
        r.vb[0:WINDOW, :] = r.vb[tm:tm + WINDOW, :]

    scale = HEAD_DIM ** -0.5
    cq3 = r.cq[...]
    sq3 = r.sq[...]
    q_groups = []
    for j in range(N_KV_HEADS):
        c0 = q[:, j * group_w:j * group_w + LANES].reshape(nb, tl, LANES)
        c1 = q[:, j * group_w + LANES:(j + 1) * group_w].reshape(nb, tl, LANES)
        r0c = (c0 * cq3 - c1 * sq3).reshape(tm, LANES)
        r1c = (c1 * cq3 + c0 * sq3).reshape(tm, LANES)
        q_groups.append(jnp.concatenate([r0c, r1c], axis=1) * scale)
    q_per_kv = N_HEADS // N_KV_HEADS
    lane = lax.broadcasted_iota(jnp.int32, (1, LANES), 1)
    low_half = lane < HEAD_DIM
    key_idx = lax.broadcasted_iota(jnp.int32, (1, band), 1)
    for blk in range(n_blocks):
        r0 = blk * n_rows
        b0 = blk * (CHUNK if not cached else band)
        for j in range(N_KV_HEADS):
            qg = q_groups[j][r0:r0 + n_rows]
            lhs = jnp.concatenate([qg * r.hmask[g:g + 1, :] for g in range(q_per_kv)],
                                  axis=0).astype(BF16)
            s = _dot_nt(lhs, r.kd[j, b0:b0 + band, :])
            if not cached and blk < WINDOW // CHUNK:
                first_valid = jnp.where(first, (WINDOW // CHUNK - blk) * CHUNK, 0)
                s = jnp.where(key_idx >= first_valid, s, NEG_INF)
            sink = jnp.concatenate(
                [jnp.full((n_rows, 1), r.sink[layer, j * q_per_kv + g], F32) for g in range(q_per_kv)], axis=0)
            m = jnp.maximum(jnp.max(s, axis=-1, keepdims=True), sink)
            pexp = jnp.exp(s - m)
            o2 = _dot(pexp.astype(BF16), r.vd[j, b0:b0 + band, :])
            num = o2[:, 0:LANES]
            den = o2[:, LANES:group_w] + jnp.exp(sink - m)
            for p in range(q_per_kv // 2):
                lo = slice((2 * p) * n_rows, (2 * p + 1) * n_rows)
                hi = slice((2 * p + 1) * n_rows, (2 * p + 2) * n_rows)
                o = jnp.where(low_half, num[lo], num[hi]) / jnp.where(low_half, den[lo], den[hi])
                col = (j * (q_per_kv // 2) + p) * LANES
                r.ao[r0:r0 + n_rows, col:col + LANES] = o.astype(BF16)
    branch_b = _dot(r.ao[...], r.wao[...])

    merged = _sigmoid(gates[:, 0:d]) * branch_a + _sigmoid(gates[:, d:2 * d]) * branch_b
    out = _dot(merged.astype(BF16), r.wout[...])
    r.xo[...] = x3 + mod[:, :, 2 * d:3 * d] * out.reshape(nb, tl, d)


def _mixer(x, mod, w, sink, shared, tables, s0_re, s0_im, past_k, past_v, *, layer, nb, tl, cached):
    b, l, d = x.shape
    n_tiles = l // tl
    n_steps = (b // nb) * n_tiles
    tm = nb * tl
    ssm_w = w["dskip"].shape[-1]
    n_state = w["pwr"].shape[-1]
    if cached:
        nch, t_steps = nb, tl
        kr = tl
        band_rows = (nb, WINDOW + tl, LANES)
        dup_rows = nb * (WINDOW + tl)
    else:
        nch, t_steps = SUBLANES, tl // SUBLANES
        kr = WINDOW
        band_rows = (WINDOW + tm, LANES)
        dup_rows = WINDOW + tm
    n_s0 = s0_re.shape[2]

    def tile_map(i):
        return (i // n_tiles, i % n_tiles, 0)

    def layer_batch_map(i):
        return (layer, i // n_tiles, 0, 0)

    def batch_map(i):
        return (i // n_tiles, 0, 0)

    pos_spec = pl.BlockSpec((1, tl, LANES), lambda i: (0, i % n_tiles, 0))
    state_in_spec = pl.BlockSpec((None, 1, n_s0, n_state), layer_batch_map)
    state_out_spec = pl.BlockSpec((1, n_s0, n_state), batch_map)
    in_specs = (
        [pl.BlockSpec((nb, tl, d), tile_map),
         pl.BlockSpec((None, nb, 1, N_ADA * d), layer_batch_map)]
        + [_layer_block(w[n], layer) for n in MIXER_LAYER_WEIGHTS]
        + [pl.BlockSpec(memory_space=pltpu.SMEM)]
        + [_whole(shared[n]) for n in MIXER_SHARED]
        + [pos_spec] * len(MIXER_TABLES)
        + [state_in_spec, state_in_spec,
           pl.BlockSpec((None, nb) + past_k.shape[2:], layer_batch_map),
           pl.BlockSpec((None, nb) + past_v.shape[2:], layer_batch_map)]
    )
    out_specs = [
        pl.BlockSpec((nb, tl, d), tile_map),
        pl.BlockSpec((nb, kr, LANES), batch_map),
        pl.BlockSpec((nb, kr, LANES), batch_map),
        state_out_spec, state_out_spec,
    ]
    out_shape = [
        jax.ShapeDtypeStruct((b, l, d), F32),
        jax.ShapeDtypeStruct((b, kr, LANES), F32),
        jax.ShapeDtypeStruct((b, kr, LANES), F32),
        jax.ShapeDtypeStruct(s0_re.shape[1:], F32),
        jax.ShapeDtypeStruct(s0_im.shape[1:], F32),
    ]
    scratch = [
        pltpu.VMEM((tm, d), BF16),
        pltpu.VMEM((ssm_w // LANES, tm, LANES), F32),
        pltpu.VMEM((tm, ssm_w), F32),
        pltpu.VMEM((tm, n_state), F32),
        pltpu.VMEM((tm, n_state), F32),
        pltpu.VMEM((ssm_w // LANES, tm, LANES), F32),
        pltpu.VMEM((nch, n_state), F32),
        pltpu.VMEM((nch, n_state), F32),
        pltpu.VMEM((1, n_state), F32),
        pltpu.VMEM((1, n_state), F32),
        pltpu.VMEM(band_rows, F32),
        pltpu.VMEM(band_rows, F32),
        pltpu.VMEM((N_KV_HEADS, dup_rows, 2 * LANES), BF16),
        pltpu.VMEM((N_KV_HEADS, dup_rows, 2 * LANES), BF16),
        pltpu.VMEM((tm, N_HEADS * HEAD_DIM), BF16),
    ]
    kern = functools.partial(_mixer_kernel, nb=nb, tl=tl, nch=nch, t_steps=t_steps,
                             n_tiles=n_tiles, layer=layer, cached=cached)
    return pl.pallas_call(
        kern,
        grid=(n_steps,),
        in_specs=in_specs,
        out_specs=out_specs,
        out_shape=out_shape,
        scratch_shapes=scratch,
        compiler_params=pltpu.CompilerParams(dimension_semantics=("arbitrary",),
                                             vmem_limit_bytes=VMEM_LIMIT_BYTES),
        name="mixer_cached" if cached else "mixer_banded",
    )(x, mod, *[w[n] for n in MIXER_LAYER_WEIGHTS], sink, *[shared[n] for n in MIXER_SHARED], *tables,
      s0_re, s0_im, past_k, past_v)


FFN_LAYER_WEIGHTS = ("g2", "wup", "cw", "cb", "wdn")


def _ffn_kernel(*refs, nb, tl, n_tiles, col_w, final):
    r = _named(("x", "mod") + FFN_LAYER_WEIGHTS + ("fg", "prev", "xo", "cs", "ub", "carry"), refs)
    tm = nb * tl
    d = r.x.shape[-1]
    d_ff = r.wdn.shape[0]
    pad = SUBLANES
    first = pl.program_id(0) % n_tiles == 0

    @pl.when(first)
    def _():
        r.carry[...] = r.prev[...]

    x3 = r.x[...]
    mod = r.mod[...]
    h3 = _rms_modulate(x3, r.g2[...], mod[:, :, 3 * d:4 * d], mod[:, :, 4 * d:5 * d])
    hb = h3.reshape(tm, d).astype(BF16)

    def conv_cols(c0):
        up = _dot(hb, r.wup[:, c0:c0 + col_w]).reshape(nb, tl, col_w)
        r.ub[:, pad - 2:pad, :] = r.carry[:, :, c0:c0 + col_w]
        r.ub[:, pad:pad + tl, :] = up
        r.carry[:, :, c0:c0 + col_w] = r.ub[:, pad + tl - 2:pad + tl, :]
        w = r.cw[:, c0:c0 + col_w]
        out = r.cb[:, c0:c0 + col_w][None] + r.ub[:, pad - 2:pad - 2 + tl, :] * w[0:1][None]
        out = out + r.ub[:, pad - 1:pad - 1 + tl, :] * w[1:2][None]
        out = out + up * w[2:3][None]
        return out.reshape(tm, col_w)

    acc = jnp.zeros((tm, d), F32)
    for c0 in range(0, d_ff, col_w):
        gate = conv_cols(d_ff + c0)
        gate = gate * _sigmoid(gate)
        val = conv_cols(c0)
        half = col_w // 2
        for a in (0, half):
            act = (gate[:, a:a + half] * val[:, a:a + half]).astype(BF16)
            acc = acc + _dot(act, r.wdn[c0 + a:c0 + a + half, :])
    r.cs[...] = r.carry[...]
    xn = x3 + mod[:, :, 5 * d:6 * d] * acc.reshape(nb, tl, d)
    if final:
        xn = xn * lax.rsqrt(jnp.mean(xn * xn, axis=-1, keepdims=True) + RMS_EPS) * r.fg[...][None]
    r.xo[...] = xn


def _ffn_col_width(d_ff):
    assert d_ff % (FFN_COLUMN_BLOCKS * 2 * LANES) == 0, d_ff
    return d_ff // FFN_COLUMN_BLOCKS


def _ffn(x, mod, w, final_g, conv_prev, *, layer, nb, tl, final):
    b, l, d = x.shape
    n_tiles = l // tl
    n_steps = (b // nb) * n_tiles
    c = w["wup"].shape[-1]
    col_w = _ffn_col_width(w["wdn"].shape[1])

    def tile_map(i):
        return (i // n_tiles, i % n_tiles, 0)

    def layer_batch_map(i):
        return (layer, i // n_tiles, 0, 0)

    kern = functools.partial(_ffn_kernel, nb=nb, tl=tl, n_tiles=n_tiles, col_w=col_w, final=final)
    return pl.pallas_call(
        kern,
        grid=(n_steps,),
        in_specs=[pl.BlockSpec((nb, tl, d), tile_map),
                  pl.BlockSpec((None, nb, 1, N_ADA * d), layer_batch_map)]
        + [_layer_block(w[n], layer) for n in FFN_LAYER_WEIGHTS]
        + [_whole(final_g),
           pl.BlockSpec((None, nb, CONV_WIDTH - 1, c), layer_batch_map)],
        out_specs=[
            pl.BlockSpec((nb, tl, d), tile_map),
            pl.BlockSpec((nb, CONV_WIDTH - 1, c), lambda i: (i // n_tiles, 0, 0)),
        ],
        out_shape=[
            jax.ShapeDtypeStruct((b, l, d), F32),
            jax.ShapeDtypeStruct((b, CONV_WIDTH - 1, c), F32),
        ],
        scratch_shapes=[
            pltpu.VMEM((nb, SUBLANES + tl, col_w), F32),
            pltpu.VMEM((nb, CONV_WIDTH - 1, c), F32),
        ],
        compiler_params=pltpu.CompilerParams(dimension_semantics=("arbitrary",),
                                             vmem_limit_bytes=VMEM_LIMIT_BYTES),
        name="conv_ffn_final" if final else "conv_ffn",
    )(x, mod, *[w[n] for n in FFN_LAYER_WEIGHTS], final_g, conv_prev)


def _rotary_tables(pos):
    half = ROPE_DIM // 2
    inv_freq = ROPE_THETA ** (-(jnp.arange(half, dtype=F32) * 2.0) / ROPE_DIM)
    ang = pos.astype(F32)[:, None] * inv_freq[None, :]
    cos = jnp.cos(ang)
    sin = jnp.sin(ang)
    n = pos.shape[0]
    rest = HEAD_DIM - ROPE_DIM
    cos_h = jnp.concatenate([cos, cos, jnp.ones((n, rest), F32)], axis=1)
    sina_h = jnp.concatenate([-sin, jnp.zeros((n, half + rest), F32)], axis=1)
    sinb_h = jnp.concatenate([jnp.zeros((n, half), F32), sin, jnp.zeros((n, rest), F32)], axis=1)
    reps = LANES // HEAD_DIM
    key_tabs = tuple(jnp.tile(t, (1, reps))[None] for t in (cos_h, sina_h, sinb_h))
    q_per_kv = N_HEADS // N_KV_HEADS
    n_rot = q_per_kv * half
    cos_q = jnp.concatenate([jnp.tile(cos, (1, q_per_kv)), jnp.ones((n, LANES - n_rot), F32)], axis=1)
    sin_q = jnp.concatenate([jnp.tile(sin, (1, q_per_kv)), jnp.zeros((n, LANES - n_rot), F32)], axis=1)
    return key_tabs + (cos_q[None], sin_q[None])


def _group_layout():
    q_per_kv = N_HEADS // N_KV_HEADS
    half = ROPE_DIM // 2
    rest = HEAD_DIM - ROPE_DIM
    n_rot = q_per_kv * half
    slots, dims = [], []
    for col in range(2):
        for lane in range(LANES):
            if lane < n_rot:
                slots.append(lane // half)
                dims.append(lane % half + half * col)
            else:
                r = lane - n_rot
                slots.append(col * (q_per_kv // 2) + r // rest)
                dims.append(ROPE_DIM + r % rest)
    return slots, dims


def _attention_constants():
    q_per_kv = N_HEADS // N_KV_HEADS
    kv_w = N_KV_HEADS * HEAD_DIM
    slots, dims = _group_layout()
    group_w = len(slots)
    q_perm = [(j * q_per_kv + slots[n]) * HEAD_DIM + dims[n] for j in range(N_KV_HEADS) for n in range(group_w)]
    q_runs, start = [], 0
    for n in range(1, len(q_perm) + 1):
        if n == len(q_perm) or q_perm[n] != q_perm[n - 1] + 1:
            q_runs.append((q_perm[start], q_perm[n - 1] + 1))
            start = n
    dims_a = jnp.asarray(dims)
    slots_a = jnp.asarray(slots)
    rows = jnp.arange(kv_w)[:, None]
    gather_k = jnp.concatenate(
        [(rows == j * HEAD_DIM + dims_a[None, :]) for j in range(N_KV_HEADS)], axis=1).astype(BF16)
    pair_dim = jnp.arange(LANES) % HEAD_DIM
    gather_v = jnp.concatenate(
        [(rows == j * HEAD_DIM + pair_dim[None, :]) for j in range(N_KV_HEADS)], axis=1).astype(BF16)
    head_mask = (slots_a[None, :] == jnp.arange(q_per_kv)[:, None]).astype(F32)
    return q_runs, dict(gk=gather_k, gv=gather_v, hmask=head_mask)


def kernel(x_prompt, x_sample, cache_k, cache_v, state_ssm_re, state_ssm_im, state_conv, c_prompt, c_sample, ada_w, ada_b, norm1_g, norm2_g, w_in, ssm_a_re, ssm_a_im, ssm_log_dt, ssm_b_re, ssm_b_im, ssm_c_re, ssm_c_im, ssm_d, w_glu, attn_sink, w_attn_o, w_out, ffn_w_up, ffn_conv_w, ffn_conv_b, ffn_w_down, final_g):
    bp, lp, d = x_prompt.shape
    bs, ls, _ = x_sample.shape
    depth = w_in.shape[0]
    groups, n_p = ssm_a_re.shape[1], ssm_a_re.shape[2]
    n_state = groups * n_p
    ssm_w = ssm_d.shape[-1]
    attn_w = N_HEADS * HEAD_DIM
    kv_w = N_KV_HEADS * HEAD_DIM
    c_ff = ffn_w_up.shape[-1]
    n_qkv = ssm_w + attn_w + 2 * kv_w
    tl_p = min(PROMPT_TILE, lp)
    n_pow = max(tl_p // SUBLANES, ls)

    mod = _ada_modulation(jnp.concatenate([c_prompt, c_sample], axis=0), ada_w, ada_b)
    mod_p = mod[:, :bp].reshape(depth, bp, 1, N_ADA * d)
    mod_s = mod[:, bp:].reshape(depth, bs, 1, N_ADA * d)
    bbar_re, bbar_im, pw_re, pw_im = _discretise(ssm_a_re, ssm_a_im, ssm_log_dt, ssm_b_re, ssm_b_im, n_pow)
    gpb = MXU_DIM // SSM_GROUP
    q_runs, attn_shared = _attention_constants()

    w_mix = dict(
        g1=norm1_g[:, None, :],
        win=jnp.concatenate([w_in[:, :, :ssm_w]] + [w_in[:, :, ssm_w + a:ssm_w + b] for a, b in q_runs]
                            + [w_in[:, :, ssm_w + attn_w:n_qkv]], axis=2).astype(BF16),
        wgate=w_in[:, :, n_qkv:].astype(BF16),
        bre=_block_diag_in(bbar_re, gpb),
        bim=_block_diag_in(bbar_im, gpb),
        cre=_block_diag_out(ssm_c_re, gpb),
        cim=_block_diag_out(ssm_c_im, gpb),
        pwr=pw_re.reshape(depth, n_pow, n_state),
        pwi=pw_im.reshape(depth, n_pow, n_state),
        dskip=ssm_d[:, None, :],
        wglu=w_glu.astype(BF16),
        wao=w_attn_o.astype(BF16),
        wout=w_out.astype(BF16),
    )
    w_ffn = dict(
        g2=norm2_g[:, None, :],
        wup=ffn_w_up.astype(BF16),
        cw=ffn_conv_w,
        cb=ffn_conv_b[:, None, :],
        wdn=ffn_w_down.astype(BF16),
    )
    fg = final_g[None]

    tabs_p = _rotary_tables(jnp.arange(lp))
    tabs_s = _rotary_tables(PAST_LEN + jnp.arange(ls))
    zero_state = jnp.zeros((depth, bp, 1, n_state), F32)
    zero_conv = jnp.zeros((depth, bp, CONV_WIDTH - 1, c_ff), F32)
    zero_past = jnp.zeros((depth, bp, SUBLANES, LANES), F32)
    s0_re = state_ssm_re.reshape(depth, 1, bs, n_state)
    s0_im = state_ssm_im.reshape(depth, 1, bs, n_state)
    past_k = cache_k.reshape(depth, bs, WINDOW, kv_w)
    past_v = cache_v.reshape(depth, bs, WINDOW, kv_w)

    xp, xs = x_prompt, x_sample
    outs_p = [[] for _ in range(5)]
    outs_s = [[] for _ in range(5)]
    for i in range(depth):
        last = i == depth - 1
        xp, nk, nv, sr, si = _mixer(xp, mod_p, w_mix, attn_sink, attn_shared, tabs_p, zero_state, zero_state,
                                    zero_past, zero_past, layer=i, nb=1, tl=tl_p, cached=False)
        xp, cs = _ffn(xp, mod_p, w_ffn, fg, zero_conv, layer=i, nb=1, tl=tl_p, final=last)
        for lst, val in zip(outs_p, (
                nk.reshape(bp, WINDOW, N_KV_HEADS, HEAD_DIM), nv.reshape(bp, WINDOW, N_KV_HEADS, HEAD_DIM),
                sr.reshape(bp, groups, n_p), si.reshape(bp, groups, n_p), cs)):
            lst.append(val)

        xs, nk, nv, sr, si = _mixer(xs, mod_s, w_mix, attn_sink, attn_shared, tabs_s, s0_re, s0_im,
                                    past_k, past_v, layer=i, nb=bs, tl=ls, cached=True)
        xs, cs = _ffn(xs, mod_s, w_ffn, fg, state_conv, layer=i, nb=bs, tl=ls, final=last)
        for lst, val in zip(outs_s, (
                nk.reshape(bs, ls, N_KV_HEADS, HEAD_DIM), nv.reshape(bs, ls, N_KV_HEADS, HEAD_DIM),
                sr.reshape(bs, groups, n_p), si.reshape(bs, groups, n_p), cs)):
            lst.append(val)

    return (xp, xs) + tuple(jnp.stack(o, axis=0) for o in outs_p) + tuple(jnp.stack(o, axis=0) for o in outs_s)
```

```python
import functools
import math
import types

import jax
import jax.numpy as jnp
from jax import lax
from jax.experimental import pallas as pl
from jax.experimental.pallas import tpu as pltpu

CHUNK = 64
SSM_GROUP = 16
SSM_STATE = 64
N_HEADS = 8
N_KV_HEADS = 2
HEAD_DIM = 64
WINDOW = 128
ROPE_DIM = HEAD_DIM // 4
ROPE_THETA = 500000.0
CONV_WIDTH = 3
N_ADA = 6
RMS_EPS = 1e-6
NEG_INF = -1e30
PAST_LEN = 2048

LANES = 128
SUBLANES = 8
MXU_DIM = 256
VMEM_LIMIT_BYTES = 58 * 1024 * 1024

PROMPT_TILE = 512
SCAN_LANES = 512
FFN_COLUMN_BLOCKS = 1
BF16 = jnp.bfloat16
F32 = jnp.float32


def _dot(a, b):
    return jnp.dot(a, b, preferred_element_type=F32)


def _dot_nt(a, b):
    return lax.dot_general(a, b, (((1,), (1,)), ((), ())), preferred_element_type=F32)


def _sigmoid(x):
    return 0.5 * jnp.tanh(0.5 * x) + 0.5


def _gelu_tanh(x):
    c = math.sqrt(2.0 / math.pi)
    return 0.5 * x * (1.0 + jnp.tanh(c * (x + 0.044715 * (x * x * x))))


def _rms_modulate(x3, g_row, shift3, scale3):
    y = x3 * lax.rsqrt(jnp.mean(x3 * x3, axis=-1, keepdims=True) + RMS_EPS)
    y = y * g_row[None]
    return y * (1.0 + scale3) + shift3


def _named(names, refs):
    assert len(names) == len(refs), (len(names), len(refs))
    return types.SimpleNamespace(**dict(zip(names, refs)))


def _layer_block(a, layer):
    zeros = (0,) * (a.ndim - 1)
    return pl.BlockSpec((None,) + a.shape[1:], lambda i: (layer,) + zeros, pipeline_mode=pl.Buffered(1))


def _whole(a):
    zeros = (0,) * a.ndim
    return pl.BlockSpec(a.shape, lambda i: zeros, pipeline_mode=pl.Buffered(1))


def _ada_kernel(c_ref, w_ref, b_ref, o_ref):
    c = c_ref[...]
    a = (c * _sigmoid(c)).astype(BF16)
    o_ref[0] = _dot(a, w_ref[0].astype(BF16)) + b_ref[0]


def _ada_modulation(c_all, ada_w, ada_b):
    depth, d, e = ada_w.shape
    nb = c_all.shape[0]
    col = d
    return pl.pallas_call(
        _ada_kernel,
        grid=(depth, e // col),
        in_specs=[
            pl.BlockSpec((nb, d), lambda l, j: (0, 0)),
            pl.BlockSpec((1, d, col), lambda l, j: (l, 0, j)),
            pl.BlockSpec((1, 1, col), lambda l, j: (l, 0, j)),
        ],
        out_specs=pl.BlockSpec((1, nb, col), lambda l, j: (l, 0, j)),
        out_shape=jax.ShapeDtypeStruct((depth, nb, e), F32),
        name="ada_modulation",
    )(c_all, ada_w, ada_b.reshape(depth, 1, e))


def _discretise_kernel(are_ref, aim_ref, ldt_ref, bre_ref, bim_ref,
                       bbr_ref, bbi_ref, pwr_ref, pwi_ref, *, n_pow):
    a_re = are_ref[0]
    a_im = aim_ref[0]
    dt = jnp.exp(ldt_ref[0])
    mag = jnp.exp(a_re * dt)
    abar_re = mag * jnp.cos(a_im * dt)
    abar_im = mag * jnp.sin(a_im * dt)
    nr = abar_re - 1.0
    ni = abar_im
    den = a_re * a_re + a_im * a_im
    fr = (nr * a_re + ni * a_im) / den
    fi = (ni * a_re - nr * a_im) / den
    b_re = bre_ref[0]
    b_im = bim_ref[0]
    bbr_ref[0] = fr[:, None, :] * b_re - fi[:, None, :] * b_im
    bbi_ref[0] = fr[:, None, :] * b_im + fi[:, None, :] * b_re
    pr, pi = abar_re, abar_im
    pwr_ref[0, 0] = pr
    pwi_ref[0, 0] = pi
    for t in range(1, n_pow):
        pr, pi = abar_re * pr - abar_im * pi, abar_re * pi + abar_im * pr
        pwr_ref[0, t] = pr
        pwi_ref[0, t] = pi


def _discretise(a_re, a_im, log_dt, b_re, b_im, n_pow):
    depth, g, p = a_re.shape
    gc = b_re.shape[-1]
    b_re_t = jnp.swapaxes(b_re, 2, 3)
    b_im_t = jnp.swapaxes(b_im, 2, 3)
    spec_a = pl.BlockSpec((1, g, p), lambda l: (l, 0, 0))
    spec_b = pl.BlockSpec((1, g, gc, p), lambda l: (l, 0, 0, 0))
    spec_p = pl.BlockSpec((1, n_pow, g, p), lambda l: (l, 0, 0, 0))
    return pl.pallas_call(
        functools.partial(_discretise_kernel, n_pow=n_pow),
        grid=(depth,),
        in_specs=[spec_a, spec_a, pl.BlockSpec((1, g, 1), lambda l: (l, 0, 0)), spec_b, spec_b],
        out_specs=[spec_b, spec_b, spec_p, spec_p],
        out_shape=[jax.ShapeDtypeStruct((depth, g, gc, p), F32)] * 2
        + [jax.ShapeDtypeStruct((depth, n_pow, g, p), F32)] * 2,
        name="s5_discretise",
    )(a_re, a_im, log_dt.reshape(depth, g, 1), b_re_t, b_im_t)


def _block_diag(m, groups_per_block):
    depth, g, rows, cols = m.shape
    nb = g // groups_per_block
    m = m.reshape(depth, nb, groups_per_block, rows, cols)
    padded = [jnp.pad(m[:, :, j], ((0, 0), (0, 0), (0, 0), (j * cols, (groups_per_block - 1 - j) * cols)))
              for j in range(groups_per_block)]
    return jnp.stack(padded, axis=2).reshape(depth, nb, groups_per_block * rows, groups_per_block * cols)


def _block_diag_in(bbar, groups_per_block):
    return _block_diag(bbar.astype(BF16), groups_per_block)


def _block_diag_out(c, groups_per_block):
    return _block_diag(jnp.swapaxes(c, 2, 3).astype(BF16), groups_per_block)


MIXER_LAYER_WEIGHTS = ("g1", "win", "wgate", "bre", "bim", "cre", "cim", "pwr", "pwi", "dskip", "wglu",
                       "wao", "wout")
MIXER_SHARED = ("gk", "gv", "hmask")
MIXER_TABLES = ("cos", "sina", "sinb", "cq", "sq")
MIXER_STATE_IN = ("s0r", "s0i", "pk", "pv")
MIXER_OUT = ("xo", "ko", "vo", "sro", "sio")
MIXER_SCRATCH = ("h", "u", "up", "xr", "xi", "yp", "sinr", "sini", "carr", "cari", "kb", "vb", "kd", "vd", "ao")


def _mixer_kernel(*refs, nb, tl, nch, t_steps, n_tiles, layer, cached):
    r = _named(("x", "mod") + MIXER_LAYER_WEIGHTS + ("sink",) + MIXER_SHARED + MIXER_TABLES + MIXER_STATE_IN
               + MIXER_OUT + MIXER_SCRATCH, refs)
    tm = nb * tl
    d = r.x.shape[-1]
    ssm_w = r.up.shape[-1]
    n_state = r.xr.shape[-1]
    attn_w = N_HEADS * HEAD_DIM
    kv_w = N_KV_HEADS * HEAD_DIM
    first = pl.program_id(0) % n_tiles == 0

    x3 = r.x[...]
    mod = r.mod[...]
    h3 = _rms_modulate(x3, r.g1[...], mod[:, :, 0:d], mod[:, :, d:2 * d])
    hb = h3.reshape(tm, d).astype(BF16)
    r.h[...] = hb
    u = _dot(hb, r.win[:, 0:ssm_w])
    for c in range(ssm_w // LANES):
        r.u[c] = u[:, c * LANES:(c + 1) * LANES]
    q = _dot(hb, r.win[:, ssm_w:ssm_w + attn_w])
    kv = _dot(hb, r.win[:, ssm_w + attn_w:ssm_w + attn_w + 2 * kv_w])
    k = kv[:, 0:kv_w]
    v = kv[:, kv_w:2 * kv_w]

    for t in range(t_steps):
        for c in range(ssm_w // LANES):
            r.up[t * nch:(t + 1) * nch, c * LANES:(c + 1) * LANES] = r.u[c, pl.ds(t, nch, stride=t_steps), :]
    n_in_blocks = r.bre.shape[0]
    kin = ssm_w // n_in_blocks
    nin = n_state // n_in_blocks
    for j in range(n_in_blocks):
        ub = r.up[:, j * kin:(j + 1) * kin].astype(BF16)
        r.xr[:, j * nin:(j + 1) * nin] = _dot(ub, r.bre[j])
        r.xi[:, j * nin:(j + 1) * nin] = _dot(ub, r.bim[j])

    def scan_block(lo, s_r, s_i, store):
        a_r = jnp.broadcast_to(r.pwr[0:1, lo:lo + SCAN_LANES], (nch, SCAN_LANES))
        a_i = jnp.broadcast_to(r.pwi[0:1, lo:lo + SCAN_LANES], (nch, SCAN_LANES))
        for t in range(t_steps):
            rows = slice(t * nch, (t + 1) * nch)
            in_r = r.xr[rows, lo:lo + SCAN_LANES]
            in_i = r.xi[rows, lo:lo + SCAN_LANES]
            if s_r is None:
                s_r, s_i = in_r, in_i
            else:
                s_r, s_i = a_r * s_r - a_i * s_i + in_r, a_r * s_i + a_i * s_r + in_i
            if store:
                r.xr[rows, lo:lo + SCAN_LANES] = s_r
                r.xi[rows, lo:lo + SCAN_LANES] = s_i
        return s_r, s_i

    if not cached:
        @pl.when(first)
        def _():
            r.carr[...] = r.s0r[0]
            r.cari[...] = r.s0i[0]

        for lo in range(0, n_state, SCAN_LANES):
            f_r, f_i = scan_block(lo, None, None, store=False)
            at_r = r.pwr[t_steps - 1:t_steps, lo:lo + SCAN_LANES]
            at_i = r.pwi[t_steps - 1:t_steps, lo:lo + SCAN_LANES]
            c_r = r.carr[:, lo:lo + SCAN_LANES]
            c_i = r.cari[:, lo:lo + SCAN_LANES]
            for ch in range(nch):
                r.sinr[ch:ch + 1, lo:lo + SCAN_LANES] = c_r
                r.sini[ch:ch + 1, lo:lo + SCAN_LANES] = c_i
                c_r, c_i = (at_r * c_r - at_i * c_i + f_r[ch:ch + 1],
                            at_r * c_i + at_i * c_r + f_i[ch:ch + 1])
            r.carr[:, lo:lo + SCAN_LANES] = c_r
            r.cari[:, lo:lo + SCAN_LANES] = c_i
        r.sro[0] = r.carr[...]
        r.sio[0] = r.cari[...]
    else:
        r.sinr[...] = r.s0r[0]
        r.sini[...] = r.s0i[0]

    gates = _dot(r.h[...], r.wgate[...])

    for lo in range(0, n_state, SCAN_LANES):
        f_r, f_i = scan_block(lo, r.sinr[:, lo:lo + SCAN_LANES], r.sini[:, lo:lo + SCAN_LANES], store=True)
        if cached:
            r.sro[0, :, lo:lo + SCAN_LANES] = f_r
            r.sio[0, :, lo:lo + SCAN_LANES] = f_i

    n_out_blocks = r.cre.shape[0]
    kout = n_state // n_out_blocks
    nout = ssm_w // n_out_blocks
    for j in range(n_out_blocks):
        sr = r.xr[:, j * kout:(j + 1) * kout].astype(BF16)
        si = r.xi[:, j * kout:(j + 1) * kout].astype(BF16)
        yb = _dot(sr, r.cre[j]) - _dot(si, r.cim[j])
        for c in range(nout // LANES):
            r.yp[j * (nout // LANES) + c] = yb[:, c * LANES:(c + 1) * LANES]
    y = jnp.concatenate(
        [jnp.concatenate([r.yp[c, pl.ds(ch, t_steps, stride=nch), :] for ch in range(nch)], axis=0)
         for c in range(ssm_w // LANES)], axis=1)
    y = y + r.dskip[...] * u
    glu = _dot(_gelu_tanh(y).astype(BF16), r.wglu[...])
    branch_a = glu[:, 0:d] * _sigmoid(glu[:, d:2 * d])

    cos3 = r.cos[...]
    sina3 = r.sina[...]
    sinb3 = r.sinb[...]
    k_rolled_down = pltpu.roll(k, LANES - ROPE_DIM // 2, axis=1).reshape(nb, tl, LANES)
    k_rolled_up = pltpu.roll(k, ROPE_DIM // 2, axis=1).reshape(nb, tl, LANES)
    k3 = k.reshape(nb, tl, LANES) * cos3 + k_rolled_down * sina3 + k_rolled_up * sinb3
    v3 = v.reshape(nb, tl, LANES)
    kr = r.ko.shape[1]
    r.ko[...] = k3[:, tl - kr:, :]
    r.vo[...] = v3[:, tl - kr:, :]
    if not cached:
        @pl.when(first)
        def _():
            r.kb[0:WINDOW, :] = jnp.zeros((WINDOW, LANES), F32)
            r.vb[0:WINDOW, :] = jnp.zeros((WINDOW, LANES), F32)

        r.kb[WINDOW:WINDOW + tm, :] = k3.reshape(tm, LANES)
        r.vb[WINDOW:WINDOW + tm, :] = v
        kall = r.kb[...]
        vall = r.vb[...]
        band = WINDOW + CHUNK
        n_rows = CHUNK
        n_blocks = tm // CHUNK
    else:
        r.kb[:, 0:WINDOW, :] = r.pk[...]
        r.kb[:, WINDOW:WINDOW + tl, :] = k3
        r.vb[:, 0:WINDOW, :] = r.pv[...]
        r.vb[:, WINDOW:WINDOW + tl, :] = v3
        band = WINDOW + tl
        n_rows = tl
        n_blocks = nb
        kall = r.kb[...].reshape(nb * band, LANES)
        vall = r.vb[...].reshape(nb * band, LANES)
    group_w = 2 * LANES
    k_spread = _dot(kall.astype(BF16), r.gk[...]).astype(BF16)
    v_spread = _dot(vall.astype(BF16), r.gv[...]).astype(BF16)
    for j in range(N_KV_HEADS):
        r.kd[j] = k_spread[:, j * group_w:(j + 1) * group_w]
        r.vd[j, :, 0:LANES] = v_spread[:, j * LANES:(j + 1) * LANES]
        r.vd[j, :, LANES:group_w] = jnp.ones((kall.shape[0], LANES), BF16)
    if not cached:
        r.kb[0:WINDOW, :] = r.kb[tm:tm + WINDOW, :]
        r.vb[0:WINDOW, :] = r.vb[tm:tm + WINDOW, :]

    scale = HEAD_DIM ** -0.5
    cq3 = r.cq[...]
    sq3 = r.sq[...]
    q_groups = []
    for j in range(N_KV_HEADS):
        c0 = q[:, j * group_w:j * group_w + LANES].reshape(nb, tl, LANES)
        c1 = q[:, j * group_w + LANES:(j + 1) * group_w].reshape(nb, tl, LANES)
        r0c = (c0 * cq3 - c1 * sq3).reshape(tm, LANES)
        r1c = (c1 * cq3 + c0 * sq3).reshape(tm, LANES)
        q_groups.append(jnp.concatenate([r0c, r1c], axis=1) * scale)
    q_per_kv = N_HEADS // N_KV_HEADS
    lane = lax.broadcasted_iota(jnp.int32, (1, LANES), 1)
    low_half = lane < HEAD_DIM
    key_idx = lax.broadcasted_iota(jnp.int32, (1, band), 1)
    for blk in range(n_blocks):
        r0 = blk * n_rows
        b0 = blk * (CHUNK if not cached else band)
        for j in range(N_KV_HEADS):
            qg = q_groups[j][r0:r0 + n_rows]
            lhs = jnp.concatenate([qg * r.hmask[g:g + 1, :] for g in range(q_per_kv)],
                                  axis=0).astype(BF16)
            s = _dot_nt(lhs, r.kd[j, b0:b0 + band, :])
            if not cached and blk < WINDOW // CHUNK:
                first_valid = jnp.where(first, (WINDOW // CHUNK - blk) * CHUNK, 0)
                s = jnp.where(key_idx >= first_valid, s, NEG_INF)
            sink = jnp.concatenate(
                [jnp.full((n_rows, 1), r.sink[layer, j * q_per_kv + g], F32) for g in range(q_per_kv)], axis=0)
            m = jnp.maximum(jnp.max(s, axis=-1, keepdims=True), sink)
            pexp = jnp.exp(s - m)
            o2 = _dot(pexp.astype(BF16), r.vd[j, b0:b0 + band, :])
            num = o2[:, 0:LANES]
            den = o2[:, LANES:group_w] + jnp.exp(sink - m)
            for p in range(q_per_kv // 2):
                lo = slice((2 * p) * n_rows, (2 * p + 1) * n_rows)
                hi = slice((2 * p + 1) * n_rows, (2 * p + 2) * n_rows)
                o = jnp.where(low_half, num[lo], num[hi]) / jnp.where(low_half, den[lo], den[hi])
                col = (j * (q_per_kv // 2) + p) * LANES
                r.ao[r0:r0 + n_rows, col:col + LANES] = o.astype(BF16)
    branch_b = _dot(r.ao[...], r.wao[...])

    half = tm // 2
    outs = []
    for a in (0, half):
        rows = slice(a, a + half)
        merged = (_sigmoid(gates[rows, 0:d]) * branch_a[rows]
                  + _sigmoid(gates[rows, d:2 * d]) * branch_b[rows])
        outs.append(_dot(merged.astype(BF16), r.wout[...]))
    out = jnp.concatenate(outs, axis=0)
    r.xo[...] = x3 + mod[:, :, 2 * d:3 * d] * out.reshape(nb, tl, d)


def _mixer(x, mod, w, sink, shared, tables, s0_re, s0_im, past_k, past_v, *, layer, nb, tl, cached):
    b, l, d = x.shape
    n_tiles = l // tl
    n_steps = (b // nb) * n_tiles
    tm = nb * tl
    ssm_w = w["dskip"].shape[-1]
    n_state = w["pwr"].shape[-1]
    if cached:
        nch, t_steps = nb, tl
        kr = tl
        band_rows = (nb, WINDOW + tl, LANES)
        dup_rows = nb * (WINDOW + tl)
    else:
        nch, t_steps = SUBLANES, tl // SUBLANES
        kr = WINDOW
        band_rows = (WINDOW + tm, LANES)
        dup_rows = WINDOW + tm
    n_s0 = s0_re.shape[2]

    def tile_map(i):
        return (i // n_tiles, i % n_tiles, 0)

    def layer_batch_map(i):
        return (layer, i // n_tiles, 0, 0)

    def batch_map(i):
        return (i // n_tiles, 0, 0)

    pos_spec = pl.BlockSpec((1, tl, LANES), lambda i: (0, i % n_tiles, 0))
    state_in_spec = pl.BlockSpec((None, 1, n_s0, n_state), layer_batch_map)
    state_out_spec = pl.BlockSpec((1, n_s0, n_state), batch_map)
    in_specs = (
        [pl.BlockSpec((nb, tl, d), tile_map),
         pl.BlockSpec((None, nb, 1, N_ADA * d), layer_batch_map)]
        + [_layer_block(w[n], layer) for n in MIXER_LAYER_WEIGHTS]
        + [pl.BlockSpec(memory_space=pltpu.SMEM)]
        + [_whole(shared[n]) for n in MIXER_SHARED]
        + [pos_spec] * len(MIXER_TABLES)
        + [state_in_spec, state_in_spec,
           pl.BlockSpec((None, nb) + past_k.shape[2:], layer_batch_map),
           pl.BlockSpec((None, nb) + past_v.shape[2:], layer_batch_map)]
    )
    out_specs = [
        pl.BlockSpec((nb, tl, d), tile_map),
        pl.BlockSpec((nb, kr, LANES), batch_map),
        pl.BlockSpec((nb, kr, LANES), batch_map),
        state_out_spec, state_out_spec,
    ]
    out_shape = [
        jax.ShapeDtypeStruct((b, l, d), F32),
        jax.ShapeDtypeStruct((b, kr, LANES), F32),
        jax.ShapeDtypeStruct((b, kr, LANES), F32),
        jax.ShapeDtypeStruct(s0_re.shape[1:], F32),
        jax.ShapeDtypeStruct(s0_im.shape[1:], F32),
    ]
    scratch = [
        pltpu.VMEM((tm, d), BF16),
        pltpu.VMEM((ssm_w // LANES, tm, LANES), F32),
        pltpu.VMEM((tm, ssm_w), F32),
        pltpu.VMEM((tm, n_state), F32),
        pltpu.VMEM((tm, n_state), F32),
        pltpu.VMEM((ssm_w // LANES, tm, LANES), F32),
        pltpu.VMEM((nch, n_state), F32),
        pltpu.VMEM((nch, n_state), F32),
        pltpu.VMEM((1, n_state), F32),
        pltpu.VMEM((1, n_state), F32),
        pltpu.VMEM(band_rows, F32),
        pltpu.VMEM(band_rows, F32),
        pltpu.VMEM((N_KV_HEADS, dup_rows, 2 * LANES), BF16),
        pltpu.VMEM((N_KV_HEADS, dup_rows, 2 * LANES), BF16),
        pltpu.VMEM((tm, N_HEADS * HEAD_DIM), BF16),
    ]
    kern = functools.partial(_mixer_kernel, nb=nb, tl=tl, nch=nch, t_steps=t_steps,
                             n_tiles=n_tiles, layer=layer, cached=cached)
    return pl.pallas_call(
        kern,
        grid=(n_steps,),
        in_specs=in_specs,
        out_specs=out_specs,
        out_shape=out_shape,
        scratch_shapes=scratch,
        compiler_params=pltpu.CompilerParams(dimension_semantics=("arbitrary",),
                                             vmem_limit_bytes=VMEM_LIMIT_BYTES),
        name="mixer_cached" if cached else "mixer_banded",
    )(x, mod, *[w[n] for n in MIXER_LAYER_WEIGHTS], sink, *[shared[n] for n in MIXER_SHARED], *tables,
      s0_re, s0_im, past_k, past_v)


FFN_LAYER_WEIGHTS = ("g2", "wup", "cw", "cb", "wdn")


def _ffn_kernel(*refs, nb, tl, n_tiles, col_w, final):
    r = _named(("x", "mod") + FFN_LAYER_WEIGHTS + ("fg", "prev", "xo", "cs", "ub", "carry"), refs)
    tm = nb * tl
    d = r.x.shape[-1]
    d_ff = r.wdn.shape[0]
    pad = SUBLANES
    first = pl.program_id(0) % n_tiles == 0

    @pl.when(first)
    def _():
        r.carry[...] = r.prev[...]

    x3 = r.x[...]
    mod = r.mod[...]
    h3 = _rms_modulate(x3, r.g2[...], mod[:, :, 3 * d:4 * d], mod[:, :, 4 * d:5 * d])
    hb = h3.reshape(tm, d).astype(BF16)

    def conv_cols(c0):
        up = _dot(hb, r.wup[:, c0:c0 + col_w]).reshape(nb, tl, col_w)
        r.ub[:, pad - 2:pad, :] = r.carry[:, :, c0:c0 + col_w]
        r.ub[:, pad:pad + tl, :] = up
        r.carry[:, :, c0:c0 + col_w] = r.ub[:, pad + tl - 2:pad + tl, :]
        w = r.cw[:, c0:c0 + col_w]
        out = r.cb[:, c0:c0 + col_w][None] + r.ub[:, pad - 2:pad - 2 + tl, :] * w[0:1][None]
        out = out + r.ub[:, pad - 1:pad - 1 + tl, :] * w[1:2][None]
        out = out + up * w[2:3][None]
        return out.reshape(tm, col_w)

    acc = jnp.zeros((tm, d), F32)
    for c0 in range(0, d_ff, col_w):
        gate = conv_cols(d_ff + c0)
        gate = gate * _sigmoid(gate)
        val = conv_cols(c0)
        half = col_w // 2
        for a in (0, half):
            act = (gate[:, a:a + half] * val[:, a:a + half]).astype(BF16)
            acc = acc + _dot(act, r.wdn[c0 + a:c0 + a + half, :])
    r.cs[...] = r.carry[...]
    xn = x3 + mod[:, :, 5 * d:6 * d] * acc.reshape(nb, tl, d)
    if final:
        xn = xn * lax.rsqrt(jnp.mean(xn * xn, axis=-1, keepdims=True) + RMS_EPS) * r.fg[...][None]
    r.xo[...] = xn


def _ffn_col_width(d_ff):
    assert d_ff % (FFN_COLUMN_BLOCKS * 2 * LANES) == 0, d_ff
    return d_ff // FFN_COLUMN_BLOCKS


def _ffn(x, mod, w, final_g, conv_prev, *, layer, nb, tl, final):
    b, l, d = x.shape
    n_tiles = l // tl
    n_steps = (b // nb) * n_tiles
    c = w["wup"].shape[-1]
    col_w = _ffn_col_width(w["wdn"].shape[1])

    def tile_map(i):
        return (i // n_tiles, i % n_tiles, 0)

    def layer_batch_map(i):
        return (layer, i // n_tiles, 0, 0)

    kern = functools.partial(_ffn_kernel, nb=nb, tl=tl, n_tiles=n_tiles, col_w=col_w, final=final)
    return pl.pallas_call(
        kern,
        grid=(n_steps,),
        in_specs=[pl.BlockSpec((nb, tl, d), tile_map),
                  pl.BlockSpec((None, nb, 1, N_ADA * d), layer_batch_map)]
        + [_layer_block(w[n], layer) for n in FFN_LAYER_WEIGHTS]
        + [_whole(final_g),
           pl.BlockSpec((None, nb, CONV_WIDTH - 1, c), layer_batch_map)],
        out_specs=[
            pl.BlockSpec((nb, tl, d), tile_map),
            pl.BlockSpec((nb, CONV_WIDTH - 1, c), lambda i: (i // n_tiles, 0, 0)),
        ],
        out_shape=[
            jax.ShapeDtypeStruct((b, l, d), F32),
            jax.ShapeDtypeStruct((b, CONV_WIDTH - 1, c), F32),
        ],
        scratch_shapes=[
            pltpu.VMEM((nb, SUBLANES + tl, col_w), F32),
            pltpu.VMEM((nb, CONV_WIDTH - 1, c), F32),
        ],
        compiler_params=pltpu.CompilerParams(dimension_semantics=("arbitrary",),
                                             vmem_limit_bytes=VMEM_LIMIT_BYTES),
        name="conv_ffn_final" if final else "conv_ffn",
    )(x, mod, *[w[n] for n in FFN_LAYER_WEIGHTS], final_g, conv_prev)


def _rotary_tables(pos):
    half = ROPE_DIM // 2
    inv_freq = ROPE_THETA ** (-(jnp.arange(half, dtype=F32) * 2.0) / ROPE_DIM)
    ang = pos.astype(F32)[:, None] * inv_freq[None, :]
    cos = jnp.cos(ang)
    sin = jnp.sin(ang)
    n = pos.shape[0]
    rest = HEAD_DIM - ROPE_DIM
    cos_h = jnp.concatenate([cos, cos, jnp.ones((n, rest), F32)], axis=1)
    sina_h = jnp.concatenate([-sin, jnp.zeros((n, half + rest), F32)], axis=1)
    sinb_h = jnp.concatenate([jnp.zeros((n, half), F32), sin, jnp.zeros((n, rest), F32)], axis=1)
    reps = LANES // HEAD_DIM
    key_tabs = tuple(jnp.tile(t, (1, reps))[None] for t in (cos_h, sina_h, sinb_h))
    q_per_kv = N_HEADS // N_KV_HEADS
    n_rot = q_per_kv * half
    cos_q = jnp.concatenate([jnp.tile(cos, (1, q_per_kv)), jnp.ones((n, LANES - n_rot), F32)], axis=1)
    sin_q = jnp.concatenate([jnp.tile(sin, (1, q_per_kv)), jnp.zeros((n, LANES - n_rot), F32)], axis=1)
    return key_tabs + (cos_q[None], sin_q[None])


def _group_layout():
    q_per_kv = N_HEADS // N_KV_HEADS
    half = ROPE_DIM // 2
    rest = HEAD_DIM - ROPE_DIM
    n_rot = q_per_kv * half
    slots, dims = [], []
    for col in range(2):
        for lane in range(LANES):
            if lane < n_rot:
                slots.append(lane // half)
                dims.append(lane % half + half * col)
            else:
                r = lane - n_rot
                slots.append(col * (q_per_kv // 2) + r // rest)
                dims.append(ROPE_DIM + r % rest)
    return slots, dims


def _attention_constants():
    q_per_kv = N_HEADS // N_KV_HEADS
    kv_w = N_KV_HEADS * HEAD_DIM
    slots, dims = _group_layout()
    group_w = len(slots)
    q_perm = [(j * q_per_kv + slots[n]) * HEAD_DIM + dims[n] for j in range(N_KV_HEADS) for n in range(group_w)]
    q_runs, start = [], 0
    for n in range(1, len(q_perm) + 1):
        if n == len(q_perm) or q_perm[n] != q_perm[n - 1] + 1:
            q_runs.append((q_perm[start], q_perm[n - 1] + 1))
            start = n
    dims_a = jnp.asarray(dims)
    slots_a = jnp.asarray(slots)
    rows = jnp.arange(kv_w)[:, None]
    gather_k = jnp.concatenate(
        [(rows == j * HEAD_DIM + dims_a[None, :]) for j in range(N_KV_HEADS)], axis=1).astype(BF16)
    pair_dim = jnp.arange(LANES) % HEAD_DIM
    gather_v = jnp.concatenate(
        [(rows == j * HEAD_DIM + pair_dim[None, :]) for j in range(N_KV_HEADS)], axis=1).astype(BF16)
    head_mask = (slots_a[None, :] == jnp.arange(q_per_kv)[:, None]).astype(F32)
    return q_runs, dict(gk=gather_k, gv=gather_v, hmask=head_mask)


def kernel(x_prompt, x_sample, cache_k, cache_v, state_ssm_re, state_ssm_im, state_conv, c_prompt, c_sample, ada_w, ada_b, norm1_g, norm2_g, w_in, ssm_a_re, ssm_a_im, ssm_log_dt, ssm_b_re, ssm_b_im, ssm_c_re, ssm_c_im, ssm_d, w_glu, attn_sink, w_attn_o, w_out, ffn_w_up, ffn_conv_w, ffn_conv_b, ffn_w_down, final_g):
    bp, lp, d = x_prompt.shape
    bs, ls, _ = x_sample.shape
    depth = w_in.shape[0]
    groups, n_p = ssm_a_re.shape[1], ssm_a_re.shape[2]
    n_state = groups * n_p
    ssm_w = ssm_d.shape[-1]
    attn_w = N_HEADS * HEAD_DIM
    kv_w = N_KV_HEADS * HEAD_DIM
    c_ff = ffn_w_up.shape[-1]
    n_qkv = ssm_w + attn_w + 2 * kv_w
    tl_p = min(PROMPT_TILE, lp)
    n_pow = max(tl_p // SUBLANES, ls)

    mod = _ada_modulation(jnp.concatenate([c_prompt, c_sample], axis=0), ada_w, ada_b)
    mod_p = mod[:, :bp].reshape(depth, bp, 1, N_ADA * d)
    mod_s = mod[:, bp:].reshape(depth, bs, 1, N_ADA * d)
    bbar_re, bbar_im, pw_re, pw_im = _discretise(ssm_a_re, ssm_a_im, ssm_log_dt, ssm_b_re, ssm_b_im, n_pow)
    gpb = MXU_DIM // SSM_GROUP
    q_runs, attn_shared = _attention_constants()

    w_mix = dict(
        g1=norm1_g[:, None, :],
        win=jnp.concatenate([w_in[:, :, :ssm_w]] + [w_in[:, :, ssm_w + a:ssm_w + b] for a, b in q_runs]
                            + [w_in[:, :, ssm_w + attn_w:n_qkv]], axis=2).astype(BF16),
        wgate=w_in[:, :, n_qkv:].astype(BF16),
        bre=_block_diag_in(bbar_re, gpb),
        bim=_block_diag_in(bbar_im, gpb),
        cre=_block_diag_out(ssm_c_re, gpb),
        cim=_block_diag_out(ssm_c_im, gpb),
        pwr=pw_re.reshape(depth, n_pow, n_state),
        pwi=pw_im.reshape(depth, n_pow, n_state),
        dskip=ssm_d[:, None, :],
        wglu=w_glu.astype(BF16),
        wao=w_attn_o.astype(BF16),
        wout=w_out.astype(BF16),
    )
    w_ffn = dict(
        g2=norm2_g[:, None, :],
        wup=ffn_w_up.astype(BF16),
        cw=ffn_conv_w,
        cb=ffn_conv_b[:, None, :],
        wdn=ffn_w_down.astype(BF16),
    )
    fg = final_g[None]

    tabs_p = _rotary_tables(jnp.arange(lp))
    tabs_s = _rotary_tables(PAST_LEN + jnp.arange(ls))
    zero_state = jnp.zeros((depth, bp, 1, n_state), F32)
    zero_conv = jnp.zeros((depth, bp, CONV_WIDTH - 1, c_ff), F32)
    zero_past = jnp.zeros((depth, bp, SUBLANES, LANES), F32)
    s0_re = state_ssm_re.reshape(depth, 1, bs, n_state)
    s0_im = state_ssm_im.reshape(depth, 1, bs, n_state)
    past_k = cache_k.reshape(depth, bs, WINDOW, kv_w)
    past_v = cache_v.reshape(depth, bs, WINDOW, kv_w)

    xp, xs = x_prompt, x_sample
    outs_p = [[] for _ in range(5)]
    outs_s = [[] for _ in range(5)]
    for i in range(depth):
        last = i == depth - 1
        xp, nk, nv, sr, si = _mixer(xp, mod_p, w_mix, attn_sink, attn_shared, tabs_p, zero_state, zero_state,
                                    zero_past, zero_past, layer=i, nb=1, tl=tl_p, cached=False)
        xp, cs = _ffn(xp, mod_p, w_ffn, fg, zero_conv, layer=i, nb=1, tl=tl_p, final=last)
        for lst, val in zip(outs_p, (
                nk.reshape(bp, WINDOW, N_KV_HEADS, HEAD_DIM), nv.reshape(bp, WINDOW, N_KV_HEADS, HEAD_DIM),
                sr.reshape(bp, groups, n_p), si.reshape(bp, groups, n_p), cs)):
            lst.append(val)

        xs, nk, nv, sr, si = _mixer(xs, mod_s, w_mix, attn_sink, attn_shared, tabs_s, s0_re, s0_im,
                                    past_k, past_v, layer=i, nb=bs, tl=ls, cached=True)
        xs, cs = _ffn(xs, mod_s, w_ffn, fg, state_conv, layer=i, nb=bs, tl=ls, final=last)
        for lst, val in zip(outs_s, (
                nk.reshape(bs, ls, N_KV_HEADS, HEAD_DIM), nv.reshape(bs, ls, N_KV_HEADS, HEAD_DIM),
                sr.reshape(bs, groups, n_p), si.reshape(bs, groups, n_p), cs)):
            lst.append(val)

    return (xp, xs) + tuple(jnp.stack(o, axis=0) for o in outs_p) + tuple(jnp.stack(o, axis=0) for o in outs_s)
```
